```python
import jax
import jax.numpy as jnp
from jax import lax
import numpy as np

D_MODEL = 1024
BATCH = 8
SEQ = 8192
DEPTH = 2

HEAD_DIM = 64
N_HEADS_MIX = D_MODEL // HEAD_DIM
H_MOBA = N_HEADS_MIX // 4
H_NSA = N_HEADS_MIX // 2
H_SB = N_HEADS_MIX - H_MOBA - H_NSA
D_MIX = N_HEADS_MIX * HEAD_DIM
ROT_DIM = HEAD_DIM // 4
ROPE_THETA = 500000.0
Q_BLOCK = 128
MOBA_BLOCK = 256
MOBA_TOPK = 3
NSA_KV_HEADS = 2
NSA_GROUP = H_NSA // NSA_KV_HEADS
NSA_CMP_LEN = 32
NSA_CMP_STRIDE = 16
NSA_CMP_HIDDEN = 256
NSA_SLC_BLOCK = 64
NSA_SLC_TOPN = 16
NSA_WINDOW = 512
NSA_FORCE_SCORE = 1e4
MEM_TOKENS = 256
X_HEADS = 4
X_HEAD_DIM = D_MODEL // X_HEADS
D_FF = 2816
CONV_WIDTH = 3
MAX_POS_OFFSET = 4096
EPS = 1e-6
NEG = -1e30
PROJ_SIZES = (H_MOBA * HEAD_DIM,) * 3 + (H_NSA * HEAD_DIM,) + (NSA_KV_HEADS * HEAD_DIM,) * 6 + (3 * H_NSA,) + (H_SB * HEAD_DIM,) * 3
D_PROJ = sum(PROJ_SIZES)

kernel_name = 'hybrid_moba_nsa_stickbreaking_block'


def rms_norm(x, g):
    xf = x.astype(jnp.float32)
    y = xf * lax.rsqrt(jnp.mean(xf * xf, axis=-1, keepdims=True) + EPS)
    return (y * g.astype(jnp.float32)).astype(x.dtype)


def rope_partial(x, pos):
    half = ROT_DIM // 2
    inv_freq = ROPE_THETA ** (-jnp.arange(half, dtype=jnp.float32) / half)
    ang = pos.astype(jnp.float32)[:, None, :, None] * inv_freq
    cos = jnp.cos(ang).astype(x.dtype)
    sin = jnp.sin(ang).astype(x.dtype)
    x1 = x[..., :half]
    x2 = x[..., half:ROT_DIM]
    return jnp.concatenate([x1 * cos - x2 * sin, x2 * cos + x1 * sin, x[..., ROT_DIM:]], axis=-1)


def masked_softmax(s, mask):
    s = jnp.where(mask, s.astype(jnp.float32), NEG)
    return jax.nn.softmax(s, axis=-1) * mask


def heads(t, n):
    B, S, _ = t.shape
    return t.reshape(B, S, n, HEAD_DIM).transpose(0, 2, 1, 3)


def chunks_to_seq(o):
    nq, B, H, Q, D = o.shape
    return o.transpose(1, 2, 0, 3, 4).reshape(B, H, nq * Q, D)


def moba_attention(q, k, v):
    B, H, S, D = q.shape
    L = MOBA_BLOCK
    nb = -(-S // L)
    pad = nb * L - S
    kp = jnp.pad(k, ((0, 0), (0, 0), (0, pad), (0, 0))).reshape(B, H, nb, L, D)
    vp = jnp.pad(v, ((0, 0), (0, 0), (0, pad), (0, 0))).reshape(B, H, nb, L, D)
    k_mean = jnp.mean(kp, axis=3)
    kk = max(1, min(MOBA_TOPK, nb - 1))
    scale = D ** -0.5
    bi = jnp.arange(B)[:, None, None, None]
    hi = jnp.arange(H)[None, :, None, None]
    blk = jnp.arange(nb)

    def chunk(c):
        q0 = c * Q_BLOCK
        qpos = q0 + jnp.arange(Q_BLOCK)
        qc = lax.dynamic_slice_in_dim(q, q0, Q_BLOCK, axis=2)
        own = q0 // L
        gate = jnp.einsum('bhqd,bhnd->bhqn', qc, k_mean).astype(jnp.float32)
        gate = jnp.where(blk < own, gate, NEG)
        _, idx = lax.top_k(gate, kk)
        valid = idx < own
        k_sel = kp[bi, hi, idx]
        v_sel = vp[bi, hi, idx]
        s_sel = jnp.einsum('bhqd,bhqnld->bhqnl', qc, k_sel).reshape(B, H, Q_BLOCK, kk * L)
        m_sel = jnp.broadcast_to(valid[..., None], (B, H, Q_BLOCK, kk, L)).reshape(B, H, Q_BLOCK, kk * L)
        k_own = lax.dynamic_index_in_dim(kp, own, axis=2, keepdims=False)
        v_own = lax.dynamic_index_in_dim(vp, own, axis=2, keepdims=False)
        s_own = jnp.einsum('bhqd,bhld->bhql', qc, k_own)
        kpos = own * L + jnp.arange(L)
        m_own = jnp.broadcast_to(kpos[None, :] <= qpos[:, None], (B, H, Q_BLOCK, L))
        s = jnp.concatenate([s_sel, s_own], axis=-1) * scale
        m = jnp.concatenate([m_sel, m_own], axis=-1)
        p = masked_softmax(s, m).astype(v.dtype)
        p_sel = p[..., :kk * L].reshape(B, H, Q_BLOCK, kk, L)
        p_own = p[..., kk * L:]
        return jnp.einsum('bhqnl,bhqnld->bhqd', p_sel, v_sel) + jnp.einsum('bhql,bhld->bhqd', p_own, v_own)

    return chunks_to_seq(lax.map(chunk, jnp.arange(S // Q_BLOCK)))


def nsa_num_cmp(S):
    return (S - NSA_CMP_LEN) // NSA_CMP_STRIDE + 1


def nsa_compress(x, pe, w1, w2):
    B, G, S, D = x.shape
    nc = nsa_num_cmp(S)
    idx = np.arange(nc)[:, None] * NSA_CMP_STRIDE + np.arange(NSA_CMP_LEN)[None, :]
    blocks = (x[:, :, idx] + pe).reshape(B, G, nc, NSA_CMP_LEN * D)
    return jax.nn.gelu(blocks @ w1) @ w2


def nsa_overlap(S):
    nc = nsa_num_cmp(S)
    nsb = S // NSA_SLC_BLOCK
    cs = np.arange(nc) * NSA_CMP_STRIDE
    ce = cs + NSA_CMP_LEN
    bs = np.arange(nsb) * NSA_SLC_BLOCK
    be = bs + NSA_SLC_BLOCK
    ov = (cs[:, None] < be[None, :]) & (ce[:, None] > bs[None, :])
    return jnp.asarray(ov.astype(np.float32))


def nsa_attention(q, kc, vc, ks, vs, kw, vw, gates):
    B, H, S, D = q.shape
    G, R = NSA_KV_HEADS, NSA_GROUP
    nc = kc.shape[2]
    nsb = S // NSA_SLC_BLOCK
    topn = min(NSA_SLC_TOPN, nsb)
    SL = NSA_SLC_BLOCK
    cmp_end = jnp.asarray(np.arange(nc) * NSA_CMP_STRIDE + NSA_CMP_LEN - 1)
    overlap = nsa_overlap(S)
    ksb = ks.reshape(B, G, nsb, SL, D)
    vsb = vs.reshape(B, G, nsb, SL, D)
    kwp = jnp.pad(kw, ((0, 0), (0, 0), (NSA_WINDOW, 0), (0, 0)))
    vwp = jnp.pad(vw, ((0, 0), (0, 0), (NSA_WINDOW, 0), (0, 0)))
    scale = D ** -0.5
    bi = jnp.arange(B)[:, None, None, None]
    gi = jnp.arange(G)[None, :, None, None]
    jb = jnp.arange(nsb)

    def chunk(c):
        q0 = c * Q_BLOCK
        qpos = q0 + jnp.arange(Q_BLOCK)
        qg = lax.dynamic_slice_in_dim(q, q0, Q_BLOCK, axis=2).reshape(B, G, R, Q_BLOCK, D)
        gc = lax.dynamic_slice_in_dim(gates, q0, Q_BLOCK, axis=2).reshape(B, G, R, Q_BLOCK, 3)
        s_c = jnp.einsum('bgrqd,bgnd->bgrqn', qg, kc) * scale
        p_c = masked_softmax(s_c, cmp_end[None, :] <= qpos[:, None])
        o_c = jnp.einsum('bgrqn,bgnd->bgrqd', p_c.astype(vc.dtype), vc)
        imp = jnp.einsum('bgrqn,nj->bgqj', p_c, overlap)
        cur = (qpos // SL)[:, None]
        forced = (jb == 0) | (jb == cur) | (jb == cur - 1)
        imp = jnp.where(forced, NSA_FORCE_SCORE, imp)
        imp = jnp.where(jb <= cur, imp, NEG)
        _, sidx = lax.top_k(imp, topn)
        k_sel = ksb[bi, gi, sidx]
        v_sel = vsb[bi, gi, sidx]
        kpos = sidx[..., None] * SL + jnp.arange(SL)
        m_s = (kpos <= qpos[:, None, None])[:, :, None].reshape(B, G, 1, Q_BLOCK, topn * SL)
        s_s = jnp.einsum('bgrqd,bgqnld->bgrqnl', qg, k_sel).reshape(B, G, R, Q_BLOCK, topn * SL) * scale
        p_s = masked_softmax(s_s, m_s).reshape(B, G, R, Q_BLOCK, topn, SL)
        o_s = jnp.einsum('bgrqnl,bgqnld->bgrqd', p_s.astype(vs.dtype), v_sel)
        k_win = lax.dynamic_slice_in_dim(kwp, q0, Q_BLOCK + NSA_WINDOW, axis=2)
        v_win = lax.dynamic_slice_in_dim(vwp, q0, Q_BLOCK + NSA_WINDOW, axis=2)
        wpos = q0 - NSA_WINDOW + jnp.arange(Q_BLOCK + NSA_WINDOW)
        dlt = qpos[:, None] - wpos[None, :]
        m_w = (dlt >= 0) & (dlt < NSA_WINDOW) & (wpos >= 0)[None, :]
        s_w = jnp.einsum('bgrqd,bgkd->bgrqk', qg, k_win) * scale
        p_w = masked_softmax(s_w, m_w)
        o_w = jnp.einsum('bgrqk,bgkd->bgrqd', p_w.astype(vw.dtype), v_win)
        o = gc[..., 0:1] * o_c + gc[..., 1:2] * o_s + gc[..., 2:3] * o_w
        return o.reshape(B, H, Q_BLOCK, D)

    return chunks_to_seq(lax.map(chunk, jnp.arange(S // Q_BLOCK)))


def stick_breaking_attention(q, k, v):
    B, H, S, D = q.shape
    scale = D ** -0.5
    kpos = jnp.arange(S)

    def chunk(c):
        q0 = c * Q_BLOCK
        qpos = q0 + jnp.arange(Q_BLOCK)
        qc = lax.dynamic_slice_in_dim(q, q0, Q_BLOCK, axis=2)
        z = jnp.einsum('bhqd,bhkd->bhqk', qc, k).astype(jnp.float32) * scale
        causal = kpos[None, :] < qpos[:, None]
        sp = jnp.where(causal, jax.nn.softplus(z), 0.0)
        tail = lax.cumsum(sp, axis=3, reverse=True) - sp
        a = jnp.where(causal, jnp.exp(jax.nn.log_sigmoid(z) - tail), 0.0)
        return jnp.einsum('bhqk,bhkd->bhqd', a.astype(v.dtype), v)

    return chunks_to_seq(lax.map(chunk, jnp.arange(S // Q_BLOCK)))


def hybrid_mixer(h, pos, w_in, w_out, cmp_pe_k, cmp_pe_v, cmp_wk1, cmp_wk2, cmp_wv1, cmp_wv2):
    B, S, _ = h.shape
    proj = h @ w_in
    cuts = [int(c) for c in np.cumsum(PROJ_SIZES)[:-1]]
    (mq, mk, mv, nq, nkc, nvc, nks, nvs, nkw, nvw, ng, sq, sk, sv) = jnp.split(proj, cuts, axis=-1)
    o_moba = moba_attention(rope_partial(heads(mq, H_MOBA), pos), rope_partial(heads(mk, H_MOBA), pos), heads(mv, H_MOBA))
    nc = nsa_num_cmp(S)
    pos_end = pos[:, np.arange(nc) * NSA_CMP_STRIDE + NSA_CMP_LEN - 1]
    kc = rope_partial(nsa_compress(heads(nkc, NSA_KV_HEADS), cmp_pe_k, cmp_wk1, cmp_wk2), pos_end)
    vc = nsa_compress(heads(nvc, NSA_KV_HEADS), cmp_pe_v, cmp_wv1, cmp_wv2)
    gates = jax.nn.sigmoid(ng).reshape(B, S, H_NSA, 3).transpose(0, 2, 1, 3)
    o_nsa = nsa_attention(rope_partial(heads(nq, H_NSA), pos), kc, vc,
                          rope_partial(heads(nks, NSA_KV_HEADS), pos), heads(nvs, NSA_KV_HEADS),
                          rope_partial(heads(nkw, NSA_KV_HEADS), pos), heads(nvw, NSA_KV_HEADS), gates)
    o_sb = stick_breaking_attention(heads(sq, H_SB), heads(sk, H_SB), heads(sv, H_SB))
    o = jnp.concatenate([o_moba, o_nsa, o_sb], axis=1)
    return o.transpose(0, 2, 1, 3).reshape(B, S, D_MIX) @ w_out


def cross_attention(h, m, wq, wk, wv, wo):
    B, S, _ = h.shape
    M = m.shape[1]
    q = (h @ wq).reshape(B, S, X_HEADS, X_HEAD_DIM)
    k = (m @ wk).reshape(B, M, X_HEADS, X_HEAD_DIM)
    v = (m @ wv).reshape(B, M, X_HEADS, X_HEAD_DIM)
    s = jnp.einsum('bshd,bmhd->bhsm', q, k).astype(jnp.float32) * (X_HEAD_DIM ** -0.5)
    p = jax.nn.softmax(s, axis=-1).astype(v.dtype)
    o = jnp.einsum('bhsm,bmhd->bshd', p, v).reshape(B, S, D_MODEL)
    return o @ wo


def conv_ffn(h, w_up, conv_w, conv_b, w_down):
    u = h @ w_up
    a, b = u[..., :D_FF], u[..., D_FF:]
    a = lax.conv_general_dilated(a, conv_w[:, None, :].astype(a.dtype), window_strides=(1,),
                                 padding=[(CONV_WIDTH - 1, 0)], dimension_numbers=('NWC', 'WIO', 'NWC'),
                                 feature_group_count=D_FF) + conv_b
    return (jax.nn.gelu(a) * b) @ w_down


def setup_inputs(seed: int = 0) -> dict:
    key = jax.random.key(seed)
    ks = jax.random.split(key, 28)
    f32 = jnp.float32

    def w(k, shape, fan_in):
        return jax.random.normal(k, shape, f32) * fan_in ** -0.5

    def gain(k, n=D_MODEL):
        return 1.0 + 0.02 * jax.random.normal(k, (DEPTH, n), f32)

    x = jax.random.normal(ks[0], (BATCH, SEQ, D_MODEL), f32)
    mem = jax.random.normal(ks[1], (BATCH, MEM_TOKENS, D_MODEL), f32)
    offset = jax.random.randint(ks[2], (BATCH, 1), 0, MAX_POS_OFFSET, dtype=jnp.int32)
    positions = offset + jnp.arange(SEQ, dtype=jnp.int32)[None, :]
    cmp_in = NSA_CMP_LEN * HEAD_DIM
    return {
        'x': x,
        'mem': mem,
        'positions': positions,
        'norm_mix_pre': gain(ks[3]),
        'norm_mix_post': gain(ks[4]),
        'w_in': w(ks[5], (DEPTH, D_MODEL, D_PROJ), D_MODEL),
        'w_out': w(ks[6], (DEPTH, D_MIX, D_MODEL), D_MIX),
        'cmp_pe_k': 0.02 * jax.random.normal(ks[7], (DEPTH, NSA_CMP_LEN, HEAD_DIM), f32),
        'cmp_pe_v': 0.02 * jax.random.normal(ks[8], (DEPTH, NSA_CMP_LEN, HEAD_DIM), f32),
        'cmp_wk1': w(ks[9], (DEPTH, cmp_in, NSA_CMP_HIDDEN), cmp_in),
        'cmp_wk2': w(ks[10], (DEPTH, NSA_CMP_HIDDEN, HEAD_DIM), NSA_CMP_HIDDEN),
        'cmp_wv1': w(ks[11], (DEPTH, cmp_in, NSA_CMP_HIDDEN), cmp_in),
        'cmp_wv2': w(ks[12], (DEPTH, NSA_CMP_HIDDEN, HEAD_DIM), NSA_CMP_HIDDEN),
        'norm_x_pre': gain(ks[13]),
        'norm_x_post': gain(ks[14]),
        'norm_mem': gain(ks[15]),
        'x_wq': w(ks[16], (DEPTH, D_MODEL, D_MODEL), D_MODEL),
        'x_wk': w(ks[17], (DEPTH, D_MODEL, D_MODEL), D_MODEL),
        'x_wv': w(ks[18], (DEPTH, D_MODEL, D_MODEL), D_MODEL),
        'x_wo': w(ks[19], (DEPTH, D_MODEL, D_MODEL), D_MODEL),
        'norm_ffn_pre': gain(ks[20]),
        'norm_ffn_post': gain(ks[21]),
        'ffn_w_up': w(ks[22], (DEPTH, D_MODEL, 2 * D_FF), D_MODEL),
        'ffn_conv_w': w(ks[23], (DEPTH, CONV_WIDTH, D_FF), CONV_WIDTH),
        'ffn_conv_b': 0.01 * jax.random.normal(ks[24], (DEPTH, D_FF), f32),
        'ffn_w_down': w(ks[25], (DEPTH, D_FF, D_MODEL), D_FF),
    }


def reference(x, mem, positions, norm_mix_pre, norm_mix_post, w_in, w_out, cmp_pe_k, cmp_pe_v,
              cmp_wk1, cmp_wk2, cmp_wv1, cmp_wv2, norm_x_pre, norm_x_post, norm_mem,
              x_wq, x_wk, x_wv, x_wo, norm_ffn_pre, norm_ffn_post, ffn_w_up, ffn_conv_w, ffn_conv_b, ffn_w_down):
    h = x
    for l in range(DEPTH):
        a = rms_norm(h, norm_mix_pre[l])
        mix = hybrid_mixer(a, positions, w_in[l], w_out[l], cmp_pe_k[l], cmp_pe_v[l],
                           cmp_wk1[l], cmp_wk2[l], cmp_wv1[l], cmp_wv2[l])
        h = h + rms_norm(mix, norm_mix_post[l])
        a = rms_norm(h, norm_x_pre[l])
        m = rms_norm(mem, norm_mem[l])
        h = h + rms_norm(cross_attention(a, m, x_wq[l], x_wk[l], x_wv[l], x_wo[l]), norm_x_post[l])
        a = rms_norm(h, norm_ffn_pre[l])
        h = h + rms_norm(conv_ffn(a, ffn_w_up[l], ffn_conv_w[l], ffn_conv_b[l], ffn_w_down[l]), norm_ffn_post[l])
    return h
```

```python
import functools
import math

import jax
import jax.numpy as jnp
import numpy as np
from jax import lax
from jax.experimental import pallas as pl
from jax.experimental.pallas import tpu as pltpu

F32 = jnp.float32
BF16 = jnp.bfloat16

LANES = 128
HEAD_DIM = 64
ROT_HALF = 8
ROPE_THETA = 500000.0
MOBA_BLOCK = 256
MOBA_TOPK = 3
NSA_CMP_LEN = 32
NSA_CMP_STRIDE = 16
NSA_SLC_BLOCK = 64
NSA_SLC_TOPN = 16
NSA_WINDOW = 512
NSA_FORCE_SCORE = 1e4
X_HEADS = 4
EPS = 1e-6
NEG = -1e30
BELOW_NEG = -3e38
SB_SKIP_TAIL = 110.0
VMEM_LIMIT = 56 * 1024 * 1024

G_MOBA_Q, G_MOBA_K, G_MOBA_V = 0, 2, 4
G_NSA_Q = 6
G_NSA_KC, G_NSA_VC, G_NSA_KS, G_NSA_VS, G_NSA_KW, G_NSA_VW = 10, 11, 12, 13, 14, 15
G_GATE = 16
G_SB_Q, G_SB_K, G_SB_V = 17, 19, 21
N_GROUPS = 23
ROPE_GROUPS = (0, 1, 2, 3, 6, 7, 8, 9, 12, 14)
GATE_COLS = 24


def _cparams(*sem):
    return pltpu.CompilerParams(dimension_semantics=sem, vmem_limit_bytes=VMEM_LIMIT)


def _dot(a, b):
    return jnp.dot(a, b, preferred_element_type=F32)


def _dot_nt(a, b):
    return lax.dot_general(a, b, (((1,), (1,)), ((), ())), preferred_element_type=F32)


def _dot_split(a_f32, b_bf16):
    hi = a_f32.astype(BF16)
    lo = (a_f32 - hi.astype(F32)).astype(BF16)
    return _dot(hi, b_bf16) + _dot(lo, b_bf16)


def _rms(x, g):
    ms = jnp.mean(x * x, axis=-1, keepdims=True)
    return x * lax.rsqrt(ms + EPS) * g


def _gelu_tanh(x):
    c = math.sqrt(2.0 / math.pi)
    return 0.5 * x * (1.0 + jnp.tanh(c * (x + 0.044715 * (x * x * x))))


def _rope(y, c, sa, sb):
    return y * c + pltpu.roll(y, LANES - ROT_HALF, 1) * sa + pltpu.roll(y, ROT_HALF, 1) * sb


def _iota(shape, dim):
    return lax.broadcasted_iota(jnp.int32, shape, dim)


def _div_pow2(x, n):
    return lax.shift_right_logical(x, jnp.int32(int(math.log2(n))))


def _proj_kernel(x_ref, g_ref, w_ref, c_ref, sa_ref, sb_ref, qkv_ref, aux_ref):
    xn = _rms(x_ref[...], g_ref[...]).astype(BF16)
    c, sa, sb = c_ref[...], sa_ref[...], sb_ref[...]
    ncol = w_ref.shape[1]
    for j in range(0, ncol, 2 * LANES):
        wdt = min(2 * LANES, ncol - j)
        acc = _dot(xn, w_ref[:, j:j + wdt])
        for s in range(wdt // LANES):
            grp = j // LANES + s
            y = acc[:, s * LANES:(s + 1) * LANES]
            if grp in ROPE_GROUPS:
                y = _rope(y, c, sa, sb)
            if grp == G_NSA_KC:
                aux_ref[:, 0:LANES] = y
            elif grp == G_NSA_VC:
                aux_ref[:, LANES:2 * LANES] = y
            elif grp == G_GATE:
                aux_ref[:, 2 * LANES:3 * LANES] = 1.0 / (1.0 + jnp.exp(-y))
            qkv_ref[:, grp * LANES:(grp + 1) * LANES] = y.astype(BF16)


def _proj(h2, g, w, c, sa, sb, tm=512):
    t, d = h2.shape
    ncol = w.shape[1]
    return pl.pallas_call(
        _proj_kernel,
        grid=(t // tm,),
        in_specs=[
            pl.BlockSpec((tm, d), lambda i: (i, 0)),
            pl.BlockSpec((1, d), lambda i: (0, 0)),
            pl.BlockSpec((d, ncol), lambda i: (0, 0)),
            pl.BlockSpec((tm, LANES), lambda i: (i, 0)),
            pl.BlockSpec((tm, LANES), lambda i: (i, 0)),
            pl.BlockSpec((tm, LANES), lambda i: (i, 0)),
        ],
        out_specs=[
            pl.BlockSpec((tm, ncol), lambda i: (i, 0)),
            pl.BlockSpec((tm, 3 * LANES), lambda i: (i, 0)),
        ],
        out_shape=[
            jax.ShapeDtypeStruct((t, ncol), BF16),
            jax.ShapeDtypeStruct((t, 3 * LANES), F32),
        ],
        compiler_params=_cparams("parallel"),
        name="proj",
    )(h2, g, w, c, sa, sb)


def _moba_kernel(q_ref, k_ref, v_ref, o_ref, kmh_ref, kml_ref, *, nb, topk):
    c = pl.program_id(2)
    tq = q_ref.shape[1]
    L = MOBA_BLOCK

    @pl.when(c == 0)
    def _():
        kmh_ref[...] = jnp.zeros_like(kmh_ref)
        kml_ref[...] = jnp.zeros_like(kml_ref)
        for j in range(nb):
            km = jnp.mean(k_ref[0, j * L:(j + 1) * L, :].astype(F32), axis=0, keepdims=True)
            hi = km.astype(BF16)
            kmh_ref[j:j + 1, :] = hi.astype(F32)
            kml_ref[j:j + 1, :] = km - hi.astype(F32)

    lane = _iota((1, LANES), 1)
    col = _iota((tq, LANES), 1)
    colf = col.astype(F32)
    q2 = q_ref[0]
    row = _iota((tq, L), 0)
    kcol = _iota((tq, L), 1)
    kmh = kmh_ref[...].astype(BF16)
    kml = kml_ref[...].astype(BF16)

    qs, sels = [], []
    for h in range(2):
        qh = jnp.where(_div_pow2(lane, HEAD_DIM) == h, q2, jnp.zeros_like(q2))
        gate = _dot_nt(qh, kmh) + _dot_nt(qh, kml)
        g = jnp.where(col < c, gate, NEG)
        sel = jnp.zeros((tq, LANES), F32)
        for _ in range(topk):
            m = jnp.max(g, axis=1, keepdims=True)
            idx = jnp.min(jnp.where(g == m, colf, 1e9), axis=1, keepdims=True)
            hit = colf == idx
            sel = jnp.where(hit & (m > 0.5 * NEG), 1.0, sel)
            g = jnp.where(hit, BELOW_NEG, g)
        qs.append(qh * jnp.asarray(HEAD_DIM ** -0.5, BF16))
        sels.append(sel)

    k0 = pl.multiple_of(c * L, L)
    k_own = k_ref[0, pl.ds(k0, L), :]
    v_own = v_ref[0, pl.ds(k0, L), :]
    carry = []
    for h in range(2):
        s = jnp.where(kcol <= row, _dot_nt(qs[h], k_own), NEG)
        m = jnp.max(s, axis=1, keepdims=True)
        p = jnp.exp(s - m)
        carry += [m, jnp.sum(p, axis=1, keepdims=True), _dot(p.astype(BF16), v_own)]

    def body(j, carry):
        kj0 = pl.multiple_of(j * L, L)
        kj = k_ref[0, pl.ds(kj0, L), :]
        vj = v_ref[0, pl.ds(kj0, L), :]
        out = []
        for h in range(2):
            m, l, acc = carry[3 * h:3 * h + 3]
            picked = jnp.sum(jnp.where(col == j, sels[h], 0.0), axis=1, keepdims=True)
            s = jnp.where(picked > 0.5, _dot_nt(qs[h], kj), NEG)
            m_new = jnp.maximum(m, jnp.max(s, axis=1, keepdims=True))
            alpha = jnp.exp(m - m_new)
            p = jnp.exp(s - m_new)
            out += [m_new, alpha * l + jnp.sum(p, axis=1, keepdims=True),
                    alpha * acc + _dot(p.astype(BF16), vj)]
        return tuple(out)

    carry = lax.fori_loop(0, c, body, tuple(carry))
    o0 = carry[2] / carry[1]
    o1 = carry[5] / carry[4]
    o_ref[0] = jnp.where(lane < HEAD_DIM, o0, o1).astype(o_ref.dtype)


def _moba(qkv, b, s):
    nb = s // MOBA_BLOCK
    topk = max(1, min(MOBA_TOPK, nb - 1))
    tq = MOBA_BLOCK
    qkv3 = qkv.reshape(b, s, -1)
    gq, gk, gv = G_MOBA_Q, G_MOBA_K, G_MOBA_V
    return pl.pallas_call(
        functools.partial(_moba_kernel, nb=nb, topk=topk),
        grid=(b, 2, s // tq),
        in_specs=[
            pl.BlockSpec((1, tq, LANES), lambda i, p, c: (i, c, gq + p)),
            pl.BlockSpec((1, s, LANES), lambda i, p, c: (i, 0, gk + p)),
            pl.BlockSpec((1, s, LANES), lambda i, p, c: (i, 0, gv + p)),
        ],
        out_specs=pl.BlockSpec((1, tq, LANES), lambda i, p, c: (i, c, p)),
        out_shape=jax.ShapeDtypeStruct((b, s, 2 * LANES), BF16),
        scratch_shapes=[pltpu.VMEM((LANES, LANES), F32), pltpu.VMEM((LANES, LANES), F32)],
        compiler_params=_cparams("parallel", "parallel", "arbitrary"),
        name="moba",
    )(qkv3, qkv3, qkv3)


def _cmp_kernel(x_ref, pe_ref, w1_ref, w2_ref, c_ref, sa_ref, sb_ref, o_ref, *, rope):
    rows = x_ref.shape[2]
    half = w1_ref.shape[0] // 2
    outs = []
    for g in range(2):
        x = x_ref[0, g]
        xa = (x + pe_ref[0:1, :]).astype(BF16)
        xb = (x + pe_ref[1:2, :]).astype(BF16)
        a = _dot(xa, w1_ref[0:half, :])
        bm = _dot(xb, w1_ref[half:2 * half, :])
        h1 = a + pltpu.roll(bm, rows - 1, 0)
        outs.append(_dot(_gelu_tanh(h1).astype(BF16), w2_ref[...]))
    y = jnp.concatenate(outs, axis=1)
    if rope:
        y = _rope(y, c_ref[0], sa_ref[0], sb_ref[0])
    o_ref[0] = y.astype(o_ref.dtype)


def _compress(x4, pe2, w1, w2, c, sa, sb, rope):
    b, g, rows, width = x4.shape
    hid = w1.shape[1]
    return pl.pallas_call(
        functools.partial(_cmp_kernel, rope=rope),
        grid=(b,),
        in_specs=[
            pl.BlockSpec((1, g, rows, width), lambda i: (i, 0, 0, 0)),
            pl.BlockSpec((2, width), lambda i: (0, 0)),
            pl.BlockSpec((2 * width, hid), lambda i: (0, 0)),
            pl.BlockSpec((hid, HEAD_DIM), lambda i: (0, 0)),
            pl.BlockSpec((1, rows, LANES), lambda i: (i, 0, 0)),
            pl.BlockSpec((1, rows, LANES), lambda i: (i, 0, 0)),
            pl.BlockSpec((1, rows, LANES), lambda i: (i, 0, 0)),
        ],
        out_specs=pl.BlockSpec((1, rows, LANES), lambda i: (i, 0, 0)),
        out_shape=jax.ShapeDtypeStruct((b, rows, LANES), BF16),
        compiler_params=_cparams("parallel"),
        name="nsa_compress",
    )(x4, pe2, w1, w2, c, sa, sb)


def _nsa_kernel(q_ref, kc_ref, vc_ref, ks_ref, vs_ref, kw_ref, vw_ref, gt_ref, ov_ref, o_ref,
                *, topn, tk):
    g = pl.program_id(1)
    c = pl.program_id(2)
    tq = q_ref.shape[1]
    nrep = q_ref.shape[2] // HEAD_DIM
    ncp = kc_ref.shape[1]
    q0 = c * tq
    wlen = NSA_WINDOW + tq

    lane = _iota((1, LANES), 1)
    in_g = _div_pow2(lane, HEAD_DIM) == g
    q2 = q_ref[0].astype(F32)
    heads = []
    for r in range(nrep):
        qp = q2[:, (r // 2) * LANES:(r // 2 + 1) * LANES]
        qp = jnp.where(g == r % 2, qp, pltpu.roll(qp, HEAD_DIM, 1))
        heads.append(jnp.where(in_g, qp, 0.0) * HEAD_DIM ** -0.5)
    qs = jnp.concatenate(heads, axis=0).astype(BF16)
    qpos = q0 + _iota((tq, 1), 0)
    qpos_r = jnp.concatenate([qpos] * nrep, axis=0)

    s_c = _dot_nt(qs, kc_ref[0])
    cmp_end = _iota((1, ncp), 1) * NSA_CMP_STRIDE + (NSA_CMP_LEN - 1)
    mask_c = cmp_end <= qpos_r
    sm = jnp.where(mask_c, s_c, NEG)
    m = jnp.max(sm, axis=1, keepdims=True)
    e = jnp.where(mask_c, jnp.exp(sm - m), 0.0)
    l = jnp.sum(e, axis=1, keepdims=True)
    p_c = e / jnp.where(l > 0.0, l, 1.0)
    o_c = _dot(p_c.astype(BF16), vc_ref[0])

    psum = p_c[0:tq]
    for r in range(1, nrep):
        psum = psum + p_c[r * tq:(r + 1) * tq]
    imp = _dot_split(psum, ov_ref[...])
    jb = _iota((tq, LANES), 1)
    jbf = jb.astype(F32)
    cur = _div_pow2(qpos, NSA_SLC_BLOCK)
    forced = (jb == 0) | (jb == cur) | (jb == cur - 1)
    imp = jnp.where(forced, NSA_FORCE_SCORE, imp)
    imp = jnp.where(jb <= cur, imp, NEG)
    sel = jnp.zeros((tq, LANES), F32)
    for _ in range(topn):
        mx = jnp.max(imp, axis=1, keepdims=True)
        idx = jnp.min(jnp.where(imp == mx, jbf, 1e9), axis=1, keepdims=True)
        hit = jbf == idx
        sel = jnp.where(hit & (mx > 0.5 * NEG), 1.0, sel)
        imp = jnp.where(hit, BELOW_NEG, imp)
    sel_b = sel.astype(BF16)

    blk_row = _iota((LANES, tk), 0)
    kiota = _iota((1, tk), 1)

    def body(t, carry):
        m, l, acc = carry
        k0 = pl.multiple_of(t * tk, tk)
        kt = ks_ref[0, pl.ds(k0, tk), :]
        vt = vs_ref[0, pl.ds(k0, tk), :]
        kpos = k0 + kiota
        expand = jnp.where(blk_row == _div_pow2(k0 + _iota((LANES, tk), 1), NSA_SLC_BLOCK), 1.0, 0.0)
        picked = _dot(sel_b, expand.astype(BF16))
        bias = jnp.where((picked > 0.5) & (kpos <= qpos), 0.0, NEG)
        s = _dot_nt(qs, kt) + jnp.concatenate([bias] * nrep, axis=0)
        m_new = jnp.maximum(m, jnp.max(s, axis=1, keepdims=True))
        alpha = jnp.exp(m - m_new)
        p = jnp.exp(s - m_new)
        return (m_new, alpha * l + jnp.sum(p, axis=1, keepdims=True),
                alpha * acc + _dot(p.astype(BF16), vt))

    n_t = (q0 + tq + tk - 1) // tk
    init = (jnp.full((nrep * tq, 1), NEG, F32), jnp.zeros((nrep * tq, 1), F32),
            jnp.zeros((nrep * tq, LANES), F32))
    _, l_s, acc_s = lax.fori_loop(0, n_t, body, init)
    o_s = acc_s / l_s

    st = pl.multiple_of(jnp.maximum(q0 - NSA_WINDOW, 0), LANES)
    kwt = kw_ref[0, pl.ds(st, wlen), :]
    vwt = vw_ref[0, pl.ds(st, wlen), :]
    dlt = qpos_r - (st + _iota((1, wlen), 1))
    s_w = jnp.where((dlt >= 0) & (dlt < NSA_WINDOW), _dot_nt(qs, kwt), NEG)
    m_w = jnp.max(s_w, axis=1, keepdims=True)
    p_w = jnp.exp(s_w - m_w)
    o_w = _dot(p_w.astype(BF16), vwt) / jnp.sum(p_w, axis=1, keepdims=True)

    gt = gt_ref[0]
    outs = []
    for r in range(nrep):
        base = (g * nrep + r) * 3
        rows = slice(r * tq, (r + 1) * tq)
        o_r = jnp.zeros((tq, LANES), F32)
        for i, o_b in enumerate((o_c, o_s, o_w)):
            gate = jnp.sum(jnp.where(lane == base + i, gt, 0.0), axis=1, keepdims=True)
            o_r = o_r + gate * o_b[rows]
        outs.append(jnp.where(g == r % 2, o_r, pltpu.roll(o_r, HEAD_DIM, 1)))
    pairs = [jnp.where(lane < HEAD_DIM, outs[2 * i], outs[2 * i + 1]) for i in range(nrep // 2)]
    o_ref[0] = jnp.concatenate(pairs, axis=1).astype(o_ref.dtype)


def _nsa_overlap_padded(s, ncp):
    nc = (s - NSA_CMP_LEN) // NSA_CMP_STRIDE + 1
    nsb = s // NSA_SLC_BLOCK
    cs = np.arange(nc) * NSA_CMP_STRIDE
    bs = np.arange(nsb) * NSA_SLC_BLOCK
    ov = (cs[:, None] < bs[None, :] + NSA_SLC_BLOCK) & (cs[:, None] + NSA_CMP_LEN > bs[None, :])
    out = np.zeros((ncp, LANES), np.float32)
    out[:nc, :nsb] = ov
    return jnp.asarray(out, BF16)


def _nsa(qkv, aux, kc, vc, b, s, tq=128, tk=512):
    qkv3 = qkv.reshape(b, s, -1)
    aux3 = aux.reshape(b, s, -1)
    ncp = kc.shape[1]
    nsb = s // NSA_SLC_BLOCK
    assert nsb <= LANES and s >= NSA_WINDOW + tq and s % tk == 0
    topn = min(NSA_SLC_TOPN, nsb)
    ov = _nsa_overlap_padded(s, ncp)
    full = lambda grp: pl.BlockSpec((1, s, LANES), lambda i, g, c: (i, 0, grp))
    return pl.pallas_call(
        functools.partial(_nsa_kernel, topn=topn, tk=tk),
        grid=(b, 2, s // tq),
        in_specs=[
            pl.BlockSpec((1, tq, 2 * LANES), lambda i, g, c: (i, c, G_NSA_Q // 2 + g)),
            pl.BlockSpec((1, ncp, LANES), lambda i, g, c: (i, 0, 0)),
            pl.BlockSpec((1, ncp, LANES), lambda i, g, c: (i, 0, 0)),
            full(G_NSA_KS), full(G_NSA_VS), full(G_NSA_KW), full(G_NSA_VW),
            pl.BlockSpec((1, tq, LANES), lambda i, g, c: (i, c, 2)),
            pl.BlockSpec((ncp, LANES), lambda i, g, c: (0, 0)),
        ],
        out_specs=pl.BlockSpec((1, tq, 2 * LANES), lambda i, g, c: (i, c, g)),
        out_shape=jax.ShapeDtypeStruct((b, s, 4 * LANES), BF16),
        compiler_params=_cparams("parallel", "parallel", "arbitrary"),
        name="nsa",
    )(qkv3, kc, vc, qkv3, qkv3, qkv3, qkv3, aux3, ov)


def _sb_kernel(q_ref, k_ref, v_ref, o_ref):
    c = pl.program_id(2)
    tq = q_ref.shape[1]
    tk = tq
    lane = _iota((1, LANES), 1)
    q2 = q_ref[0]
    row = _iota((tq, tk), 0)
    kcol = _iota((tq, tk), 1)
    suffix = jnp.where(_iota((tk, tk), 0) >= _iota((tk, tk), 1), 1.0, 0.0).astype(BF16)

    outs = []
    for h in range(2):
        qh = jnp.where(_div_pow2(lane, HEAD_DIM) == h, q2, jnp.zeros_like(q2)) * jnp.asarray(HEAD_DIM ** -0.5, BF16)

        def cond(st):
            return (st[0] >= 0) & (st[1] < SB_SKIP_TAIL)

        def body(st):
            j, _, tail, acc = st
            k0 = pl.multiple_of(j * tk, tk)
            kj = k_ref[0, pl.ds(k0, tk), :]
            vj = v_ref[0, pl.ds(k0, tk), :]
            z = _dot_nt(qh, kj)
            causal = (j * tk + kcol) < (c * tq + row)
            sp = jnp.where(causal, jnp.maximum(z, 0.0) + jnp.log(1.0 + jnp.exp(-jnp.abs(z))), 0.0)
            incl = _dot_split(sp, suffix)
            a = jnp.where(causal, jnp.exp(z - incl - tail), 0.0)
            acc = acc + _dot(a.astype(BF16), vj)
            tail = tail + incl[:, 0:1]
            return (j - 1, jnp.min(tail), tail, acc)

        init = (c, jnp.float32(0.0), jnp.zeros((tq, 1), F32), jnp.zeros((tq, LANES), F32))
        outs.append(lax.while_loop(cond, body, init)[3])
    o_ref[0] = jnp.where(lane < HEAD_DIM, outs[0], outs[1]).astype(o_ref.dtype)


def _sb(qkv, b, s, tq=256):
    qkv3 = qkv.reshape(b, s, -1)
    gq, gk, gv = G_SB_Q, G_SB_K, G_SB_V
    return pl.pallas_call(
        _sb_kernel,
        grid=(b, 2, s // tq),
        in_specs=[
            pl.BlockSpec((1, tq, LANES), lambda i, p, c: (i, c, gq + p)),
            pl.BlockSpec((1, s, LANES), lambda i, p, c: (i, 0, gk + p)),
            pl.BlockSpec((1, s, LANES), lambda i, p, c: (i, 0, gv + p)),
        ],
        out_specs=pl.BlockSpec((1, tq, LANES), lambda i, p, c: (i, c, p)),
        out_shape=jax.ShapeDtypeStruct((b, s, 2 * LANES), BF16),
        compiler_params=_cparams("parallel", "parallel", "arbitrary"),
        name="stick_breaking",
    )(qkv3, qkv3, qkv3)


def _outproj_kernel(h_ref, om_ref, on_ref, os_ref, w_ref, g_ref, o_ref):
    dm, dn = om_ref.shape[1], on_ref.shape[1]
    mix = (_dot(om_ref[...], w_ref[0:dm, :]) + _dot(on_ref[...], w_ref[dm:dm + dn, :])
           + _dot(os_ref[...], w_ref[dm + dn:, :]))
    o_ref[...] = h_ref[...] + _rms(mix, g_ref[...])


def _outproj(h2, om, on, osb, w, g, tm=512):
    t, d = h2.shape
    row = lambda n: pl.BlockSpec((tm, n), lambda i: (i, 0))
    return pl.pallas_call(
        _outproj_kernel,
        grid=(t // tm,),
        in_specs=[row(d), row(om.shape[1]), row(on.shape[1]), row(osb.shape[1]),
                  pl.BlockSpec(w.shape, lambda i: (0, 0)), pl.BlockSpec((1, d), lambda i: (0, 0))],
        out_specs=row(d),
        out_shape=jax.ShapeDtypeStruct((t, d), F32),
        compiler_params=_cparams("parallel"),
        name="mixer_out",
    )(h2, om, on, osb, w, g)


def _memkv_kernel(m_ref, g_ref, w_ref, o_ref):
    o_ref[...] = _dot(_rms(m_ref[...], g_ref[...]).astype(BF16), w_ref[...]).astype(o_ref.dtype)


def _memkv(mem2, g, wkv, tm=256):
    t, d = mem2.shape
    n = wkv.shape[1]
    return pl.pallas_call(
        _memkv_kernel,
        grid=(t // tm,),
        in_specs=[pl.BlockSpec((tm, d), lambda i: (i, 0)), pl.BlockSpec((1, d), lambda i: (0, 0)),
                  pl.BlockSpec((d, n), lambda i: (0, 0))],
        out_specs=pl.BlockSpec((tm, n), lambda i: (i, 0)),
        out_shape=jax.ShapeDtypeStruct((t, n), BF16),
        compiler_params=_cparams("parallel"),
        name="mem_kv",
    )(mem2, g, wkv)


def _cross_kernel(h_ref, k_ref, v_ref, wq_ref, wo_ref, gpre_ref, gpost_ref, o_ref):
    h = h_ref[0]
    d = h.shape[1]
    hd = d // X_HEADS
    q = _dot(_rms(h, gpre_ref[...]).astype(BF16), wq_ref[...])
    outs = []
    for x in range(X_HEADS):
        cols = slice(x * hd, (x + 1) * hd)
        qh = (q[:, cols] * hd ** -0.5).astype(BF16)
        s = _dot_nt(qh, k_ref[0, :, cols])
        p = jnp.exp(s - jnp.max(s, axis=1, keepdims=True))
        p = p / jnp.sum(p, axis=1, keepdims=True)
        outs.append(_dot(p.astype(BF16), v_ref[0, :, cols]).astype(BF16))
    o = _dot(jnp.concatenate(outs, axis=1), wo_ref[...])
    o_ref[0] = h + _rms(o, gpost_ref[...])


def _cross(h3, kv3, wq, wo, gpre, gpost, tm=512):
    b, s, d = h3.shape
    m = kv3.shape[1]
    vec = pl.BlockSpec((1, d), lambda i, j: (0, 0))
    mat = pl.BlockSpec((d, d), lambda i, j: (0, 0))
    return pl.pallas_call(
        _cross_kernel,
        grid=(b, s // tm),
        in_specs=[pl.BlockSpec((1, tm, d), lambda i, j: (i, j, 0)),
                  pl.BlockSpec((1, m, d), lambda i, j: (i, 0, 0)),
                  pl.BlockSpec((1, m, d), lambda i, j: (i, 0, 1)),
                  mat, mat, vec, vec],
        out_specs=pl.BlockSpec((1, tm, d), lambda i, j: (i, j, 0)),
        out_shape=jax.ShapeDtypeStruct((b, s, d), F32),
        compiler_params=_cparams("parallel", "parallel"),
        name="cross_attn",
    )(h3, kv3, kv3, wq, wo, gpre, gpost)


def _ffn_kernel(h_ref, wup_ref, cw_ref, cb_ref, wdn_ref, gpre_ref, gpost_ref, o_ref, tail_ref, acc_ref,
                *, tc):
    j = pl.program_id(1)
    h = h_ref[0]
    tm = h.shape[0]
    dff = wdn_ref.shape[0]

    @pl.when(j == 0)
    def _():
        tail_ref[...] = jnp.zeros_like(tail_ref)

    xn = _rms(h, gpre_ref[...]).astype(BF16)
    row = _iota((tm, tc), 0)
    for c0 in range(0, dff, tc):
        cols = slice(c0, c0 + tc)
        a = _dot(xn, wup_ref[:, cols])
        gate = _dot(xn, wup_ref[:, dff + c0:dff + c0 + tc])
        prev1 = tail_ref[7:8, cols]
        prev2 = tail_ref[6:7, cols]
        a1 = jnp.where(row == 0, prev1, pltpu.roll(a, 1, 0))
        a2 = jnp.where(row == 0, prev2, jnp.where(row == 1, prev1, pltpu.roll(a, 2, 0)))
        tail_ref[:, cols] = a[tm - 8:tm, :]
        y = cw_ref[0:1, cols] * a2 + cw_ref[1:2, cols] * a1 + cw_ref[2:3, cols] * a + cb_ref[:, cols]
        part = _dot((_gelu_tanh(y) * gate).astype(BF16), wdn_ref[cols, :])
        if c0 == 0:
            acc_ref[...] = part
        else:
            acc_ref[...] += part
    o_ref[0] = h + _rms(acc_ref[...], gpost_ref[...])


def _ffn(h3, wup, cw, cb, wdn, gpre, gpost, tm=512, tc=256):
    b, s, d = h3.shape
    dff = wdn.shape[0]
    const = lambda shape: pl.BlockSpec(shape, lambda i, j: (0, 0))
    return pl.pallas_call(
        functools.partial(_ffn_kernel, tc=tc),
        grid=(b, s // tm),
        in_specs=[pl.BlockSpec((1, tm, d), lambda i, j: (i, j, 0)),
                  const(wup.shape), const(cw.shape), const(cb.shape), const(wdn.shape),
                  const((1, d)), const((1, d))],
        out_specs=pl.BlockSpec((1, tm, d), lambda i, j: (i, j, 0)),
        out_shape=jax.ShapeDtypeStruct((b, s, d), F32),
        scratch_shapes=[pltpu.VMEM((8, dff), F32), pltpu.VMEM((tm, d), F32)],
        compiler_params=_cparams("parallel", "arbitrary"),
        name="conv_ffn",
    )(h3, wup, cw, cb, wdn, gpre, gpost)


def _rope_tables(pos):
    inv_freq = ROPE_THETA ** (-jnp.arange(ROT_HALF, dtype=F32) / ROT_HALF)
    ang = pos.astype(F32)[..., None] * inv_freq
    cos, sin = jnp.cos(ang), jnp.sin(ang)
    rest = HEAD_DIM - 2 * ROT_HALF
    ones = jnp.ones(pos.shape + (rest,), F32)
    zeros = jnp.zeros(pos.shape + (rest,), F32)
    z8 = jnp.zeros_like(sin)
    two = lambda t: jnp.concatenate([t, t], axis=-1)
    return (two(jnp.concatenate([cos, cos, ones], -1)),
            two(jnp.concatenate([-sin, z8, zeros], -1)),
            two(jnp.concatenate([z8, sin, zeros], -1)))


def _pad_gate_cols(w_in):
    cut = G_GATE * LANES
    pad = jnp.zeros((w_in.shape[0], LANES - GATE_COLS), w_in.dtype)
    return jnp.concatenate([w_in[:, :cut + GATE_COLS], pad, w_in[:, cut + GATE_COLS:]], axis=1)


def kernel(x, mem, positions, norm_mix_pre, norm_mix_post, w_in, w_out, cmp_pe_k, cmp_pe_v, cmp_wk1, cmp_wk2, cmp_wv1, cmp_wv2, norm_x_pre, norm_x_post, norm_mem, x_wq, x_wk, x_wv, x_wo, norm_ffn_pre, norm_ffn_post, ffn_w_up, ffn_conv_w, ffn_conv_b, ffn_w_down):
    b, s, d = x.shape
    depth = w_in.shape[0]
    t = b * s
    rows = s // NSA_CMP_STRIDE
    c_tok, sa_tok, sb_tok = (tb.reshape(t, LANES) for tb in _rope_tables(positions))
    end_idx = jnp.minimum(jnp.arange(rows) * NSA_CMP_STRIDE + NSA_CMP_LEN - 1, s - 1)
    c_end, sa_end, sb_end = _rope_tables(positions[:, end_idx])
    vec = lambda g: g.reshape(1, -1)

    def blocks16(a):
        return a.reshape(b, s, 2, HEAD_DIM).transpose(0, 2, 1, 3).reshape(b, 2, rows, NSA_CMP_STRIDE * HEAD_DIM)

    h = x
    for l in range(depth):
        qkv, aux = _proj(h.reshape(t, d), vec(norm_mix_pre[l]), _pad_gate_cols(w_in[l]).astype(BF16),
                         c_tok, sa_tok, sb_tok)
        aux3 = aux.reshape(b, s, 3 * LANES)
        pe_k = cmp_pe_k[l].reshape(2, -1)
        pe_v = cmp_pe_v[l].reshape(2, -1)
        kc = _compress(blocks16(aux3[..., 0:LANES]), pe_k, cmp_wk1[l].astype(BF16), cmp_wk2[l].astype(BF16),
                       c_end, sa_end, sb_end, True)
        vc = _compress(blocks16(aux3[..., LANES:2 * LANES]), pe_v, cmp_wv1[l].astype(BF16),
                       cmp_wv2[l].astype(BF16), c_end, sa_end, sb_end, False)
        o_moba = _moba(qkv, b, s)
        o_nsa = _nsa(qkv, aux, kc, vc, b, s)
        o_sb = _sb(qkv, b, s)
        h = _outproj(h.reshape(t, d), o_moba.reshape(t, -1), o_nsa.reshape(t, -1), o_sb.reshape(t, -1),
                     w_out[l].astype(BF16), vec(norm_mix_post[l]))
        wkv = jnp.concatenate([x_wk[l], x_wv[l]], axis=1).astype(BF16)
        kv = _memkv(mem.reshape(-1, d), vec(norm_mem[l]), wkv).reshape(b, -1, 2 * d)
        h = _cross(h.reshape(b, s, d), kv, x_wq[l].astype(BF16), x_wo[l].astype(BF16),
                   vec(norm_x_pre[l]), vec(norm_x_post[l]))
        h = _ffn(h, ffn_w_up[l].astype(BF16), ffn_conv_w[l], vec(ffn_conv_b[l]), ffn_w_down[l].astype(BF16),
                 vec(norm_ffn_pre[l]), vec(norm_ffn_post[l]))
    return h
```

```python
import functools
import math

import jax
import jax.numpy as jnp
import numpy as np
from jax import lax
from jax.experimental import pallas as pl
from jax.experimental.pallas import tpu as pltpu

F32 = jnp.float32
BF16 = jnp.bfloat16

LANES = 128
HEAD_DIM = 64
ROT_HALF = 8
ROPE_THETA = 500000.0
MOBA_BLOCK = 256
MOBA_TOPK = 3
NSA_CMP_LEN = 32
NSA_CMP_STRIDE = 16
NSA_SLC_BLOCK = 64
NSA_SLC_TOPN = 16
NSA_WINDOW = 512
NSA_FORCE_SCORE = 1e4
X_HEADS = 4
EPS = 1e-6
NEG = -1e30
BELOW_NEG = -3e38
SB_SKIP_TAIL = 110.0
VMEM_LIMIT = 56 * 1024 * 1024

G_MOBA_Q, G_MOBA_K, G_MOBA_V = 0, 2, 4
G_NSA_Q = 6
G_NSA_KC, G_NSA_VC, G_NSA_KS, G_NSA_VS, G_NSA_KW, G_NSA_VW = 10, 11, 12, 13, 14, 15
G_GATE = 16
G_SB_Q, G_SB_K, G_SB_V = 17, 19, 21
N_GROUPS = 23
ROPE_GROUPS = (0, 1, 2, 3, 6, 7, 8, 9, 12, 14)
GATE_COLS = 24


def _cparams(*sem):
    return pltpu.CompilerParams(dimension_semantics=sem, vmem_limit_bytes=VMEM_LIMIT)


def _dot(a, b):
    return jnp.dot(a, b, preferred_element_type=F32)


def _dot_nt(a, b):
    return lax.dot_general(a, b, (((1,), (1,)), ((), ())), preferred_element_type=F32)


def _dot_split(a_f32, b_bf16):
    hi = a_f32.astype(BF16)
    lo = (a_f32 - hi.astype(F32)).astype(BF16)
    return _dot(hi, b_bf16) + _dot(lo, b_bf16)


def _rms(x, g):
    ms = jnp.mean(x * x, axis=-1, keepdims=True)
    return x * lax.rsqrt(ms + EPS) * g


def _gelu_tanh(x):
    c = math.sqrt(2.0 / math.pi)
    return 0.5 * x * (1.0 + jnp.tanh(c * (x + 0.044715 * (x * x * x))))


def _rope(y, c, sa, sb):
    return y * c + pltpu.roll(y, LANES - ROT_HALF, 1) * sa + pltpu.roll(y, ROT_HALF, 1) * sb


def _iota(shape, dim):
    return lax.broadcasted_iota(jnp.int32, shape, dim)


def _div_pow2(x, n):
    return lax.shift_right_logical(x, jnp.int32(int(math.log2(n))))


def _proj_kernel(x_ref, g_ref, w_ref, c_ref, sa_ref, sb_ref, qkv_ref, aux_ref):
    xn = _rms(x_ref[...], g_ref[...]).astype(BF16)
    c, sa, sb = c_ref[...], sa_ref[...], sb_ref[...]
    ncol = w_ref.shape[1]
    for j in range(0, ncol, 2 * LANES):
        wdt = min(2 * LANES, ncol - j)
        acc = _dot(xn, w_ref[:, j:j + wdt])
        for s in range(wdt // LANES):
            grp = j // LANES + s
            y = acc[:, s * LANES:(s + 1) * LANES]
            if grp in ROPE_GROUPS:
                y = _rope(y, c, sa, sb)
            if grp == G_NSA_KC:
                aux_ref[:, 0:LANES] = y
            elif grp == G_NSA_VC:
                aux_ref[:, LANES:2 * LANES] = y
            elif grp == G_GATE:
                aux_ref[:, 2 * LANES:3 * LANES] = 1.0 / (1.0 + jnp.exp(-y))
            qkv_ref[:, grp * LANES:(grp + 1) * LANES] = y.astype(BF16)


def _proj(h2, g, w, c, sa, sb, tm=512):
    t, d = h2.shape
    ncol = w.shape[1]
    return pl.pallas_call(
        _proj_kernel,
        grid=(t // tm,),
        in_specs=[
            pl.BlockSpec((tm, d), lambda i: (i, 0)),
            pl.BlockSpec((1, d), lambda i: (0, 0)),
            pl.BlockSpec((d, ncol), lambda i: (0, 0)),
            pl.BlockSpec((tm, LANES), lambda i: (i, 0)),
            pl.BlockSpec((tm, LANES), lambda i: (i, 0)),
            pl.BlockSpec((tm, LANES), lambda i: (i, 0)),
        ],
        out_specs=[
            pl.BlockSpec((tm, ncol), lambda i: (i, 0)),
            pl.BlockSpec((tm, 3 * LANES), lambda i: (i, 0)),
        ],
        out_shape=[
            jax.ShapeDtypeStruct((t, ncol), BF16),
            jax.ShapeDtypeStruct((t, 3 * LANES), F32),
        ],
        compiler_params=_cparams("parallel"),
        name="proj",
    )(h2, g, w, c, sa, sb)


def _topk_rows(score, k):
    rowf = _iota(score.shape, 0).astype(F32)
    sel = jnp.zeros(score.shape, F32)
    for _ in range(k):
        m = jnp.max(score, axis=0, keepdims=True)
        idx = jnp.min(jnp.where(score == m, rowf, 1e9), axis=0, keepdims=True)
        hit = rowf == idx
        sel = jnp.where(hit & (m > 0.5 * NEG), 1.0, sel)
        score = jnp.where(hit, BELOW_NEG, score)
    return sel


def _moba_kernel(q_ref, k_ref, vt_ref, o_ref, km_ref, bias_ref, acc_ref, *, nb, topk, per_tile):
    c = pl.program_id(2)
    tq = q_ref.shape[1]
    L = MOBA_BLOCK

    @pl.when(c == 0)
    def _():
        km_ref[...] = jnp.zeros_like(km_ref)
        for j in range(nb):
            km = jnp.mean(k_ref[0, j * L:(j + 1) * L, :].astype(F32), axis=0, keepdims=True)
            hi = km.astype(BF16).astype(F32)
            km_ref[0, j:j + 1, :] = hi
            km_ref[1, j:j + 1, :] = km - hi

    drow = _iota((LANES, 1), 0)
    q2t = q_ref[0].astype(F32).T
    blk = _iota((km_ref.shape[1], tq), 0)
    kmh = km_ref[0].astype(BF16)
    kml = km_ref[1].astype(BF16)
    qts = []
    for h in range(2):
        qt = jnp.where(_div_pow2(drow, HEAD_DIM) == h, q2t, 0.0).astype(BF16)
        gate = _dot(kmh, qt) + _dot(kml, qt)
        sel = _topk_rows(jnp.where(blk < c, gate, NEG), topk)
        bias_ref[h] = jnp.where(sel > 0.5, 0.0, NEG)
        qts.append(qt * jnp.asarray(HEAD_DIM ** -0.5, BF16))

    causal = jnp.where(_iota((L, tq), 0) <= _iota((L, tq), 1), 0.0, NEG)
    acc_ref[...] = jnp.zeros_like(acc_ref)

    def body(t, carry):
        kt = k_ref[0, pl.ds(pl.multiple_of(t * (per_tile * L), per_tile * L), per_tile * L), :]
        vtt = vt_ref[0, 0, t]
        out = []
        for h in range(2):
            m, l = carry[2 * h:2 * h + 2]
            bias = jnp.concatenate(
                [jnp.where(t * per_tile + i == c, causal,
                           jnp.broadcast_to(bias_ref[h, pl.ds(t * per_tile + i, 1), :], (L, tq)))
                 for i in range(per_tile)], axis=0)
            st = _dot(kt, qts[h]) + bias
            m_new = jnp.maximum(m, jnp.max(st, axis=0, keepdims=True))
            alpha = jnp.exp(m - m_new)
            p = jnp.exp(st - m_new)
            acc_ref[h] = alpha * acc_ref[h] + _dot(vtt, p.astype(BF16))
            out += [m_new, alpha * l + jnp.sum(p, axis=0, keepdims=True)]
        return tuple(out)

    init = (jnp.full((1, tq), NEG, F32), jnp.zeros((1, tq), F32)) * 2
    carry = lax.fori_loop(0, c // per_tile + 1, body, init)
    ot = jnp.where(drow < HEAD_DIM, acc_ref[0] / carry[1], acc_ref[1] / carry[3])
    o_ref[0] = ot.T.astype(o_ref.dtype)


def _key_major(a3, blk):
    b, s, c = a3.shape
    return a3.reshape(b, s // blk, blk, c // LANES, LANES).transpose(0, 3, 1, 4, 2)


def _moba(qkv, b, s, per_tile=4):
    nb = s // MOBA_BLOCK
    nbp = -(-nb // 8) * 8
    assert nb % per_tile == 0
    topk = max(1, min(MOBA_TOPK, nb - 1))
    tq = MOBA_BLOCK
    tk = per_tile * MOBA_BLOCK
    qkv3 = qkv.reshape(b, s, -1)
    gq, gk, gv = G_MOBA_Q, G_MOBA_K, G_MOBA_V
    vt = _key_major(qkv3[..., gv * LANES:(gv + 2) * LANES], tk)
    return pl.pallas_call(
        functools.partial(_moba_kernel, nb=nb, topk=topk, per_tile=per_tile),
        grid=(b, 2, s // tq),
        in_specs=[
            pl.BlockSpec((1, tq, LANES), lambda i, p, c: (i, c, gq + p)),
            pl.BlockSpec((1, s, LANES), lambda i, p, c: (i, 0, gk + p)),
            pl.BlockSpec((1, 1, s // tk, LANES, tk), lambda i, p, c: (i, p, 0, 0, 0)),
        ],
        out_specs=pl.BlockSpec((1, tq, LANES), lambda i, p, c: (i, c, p)),
        out_shape=jax.ShapeDtypeStruct((b, s, 2 * LANES), BF16),
        scratch_shapes=[pltpu.VMEM((2, nbp, LANES), F32), pltpu.VMEM((2, nbp, tq), F32),
                        pltpu.VMEM((2, LANES, tq), F32)],
        compiler_params=_cparams("parallel", "parallel", "arbitrary"),
        name="moba",
    )(qkv3, qkv3, vt)


def _cmp_kernel(x_ref, pe_ref, w1_ref, w2_ref, c_ref, sa_ref, sb_ref, o_ref, *, rope):
    rows = x_ref.shape[2]
    half = w1_ref.shape[0] // 2
    outs = []
    for g in range(2):
        x = x_ref[0, g]
        xa = (x + pe_ref[0:1, :]).astype(BF16)
        xb = (x + pe_ref[1:2, :]).astype(BF16)
        a = _dot(xa, w1_ref[0:half, :])
        bm = _dot(xb, w1_ref[half:2 * half, :])
        h1 = a + pltpu.roll(bm, rows - 1, 0)
        outs.append(_dot(_gelu_tanh(h1).astype(BF16), w2_ref[...]))
    y = jnp.concatenate(outs, axis=1)
    if rope:
        y = _rope(y, c_ref[0], sa_ref[0], sb_ref[0])
    o_ref[0] = y.astype(o_ref.dtype)


def _compress(x4, pe2, w1, w2, c, sa, sb, rope):
    b, g, rows, width = x4.shape
    hid = w1.shape[1]
    return pl.pallas_call(
        functools.partial(_cmp_kernel, rope=rope),
        grid=(b,),
        in_specs=[
            pl.BlockSpec((1, g, rows, width), lambda i: (i, 0, 0, 0)),
            pl.BlockSpec((2, width), lambda i: (0, 0)),
            pl.BlockSpec((2 * width, hid), lambda i: (0, 0)),
            pl.BlockSpec((hid, HEAD_DIM), lambda i: (0, 0)),
            pl.BlockSpec((1, rows, LANES), lambda i: (i, 0, 0)),
            pl.BlockSpec((1, rows, LANES), lambda i: (i, 0, 0)),
            pl.BlockSpec((1, rows, LANES), lambda i: (i, 0, 0)),
        ],
        out_specs=pl.BlockSpec((1, rows, LANES), lambda i: (i, 0, 0)),
        out_shape=jax.ShapeDtypeStruct((b, rows, LANES), BF16),
        compiler_params=_cparams("parallel"),
        name="nsa_compress",
    )(x4, pe2, w1, w2, c, sa, sb)


def _nsa_kernel(q_ref, kc_ref, vct_ref, ks_ref, vst_ref, kw_ref, vwt_ref, gt_ref, ovt_ref, o_ref,
                bias_ref, gts_ref, acc_ref, *, topn, tk):
    g = pl.program_id(1)
    c = pl.program_id(2)
    tq = q_ref.shape[1]
    nrep = q_ref.shape[2] // HEAD_DIM
    ncp = kc_ref.shape[1]
    q0 = c * tq
    nwin = (NSA_WINDOW + tq) // LANES
    per_tile = tk // NSA_SLC_BLOCK

    drow = _iota((LANES, 1), 0)
    in_g = _div_pow2(drow, HEAD_DIM) == g
    q2 = q_ref[0].astype(F32)
    cols = []
    for r in range(nrep):
        qpt = q2[:, (r // 2) * LANES:(r // 2 + 1) * LANES].T
        qpt = jnp.where(g == r % 2, qpt, pltpu.roll(qpt, HEAD_DIM, 0))
        cols.append(jnp.where(in_g, qpt, 0.0) * HEAD_DIM ** -0.5)
    qst = jnp.concatenate(cols, axis=1).astype(BF16)
    qpos = q0 + _iota((1, tq), 1)
    qpos_w = jnp.concatenate([qpos] * nrep, axis=1)

    s_c = _dot(kc_ref[0], qst)
    cmp_end = _iota((ncp, 1), 0) * NSA_CMP_STRIDE + (NSA_CMP_LEN - 1)
    mask_c = cmp_end <= qpos_w
    sm = jnp.where(mask_c, s_c, NEG)
    m = jnp.max(sm, axis=0, keepdims=True)
    e = jnp.where(mask_c, jnp.exp(sm - m), 0.0)
    l = jnp.sum(e, axis=0, keepdims=True)
    p_c = e / jnp.where(l > 0.0, l, 1.0)
    o_c = _dot(vct_ref[0], p_c.astype(BF16))

    psum = p_c[:, 0:tq]
    for r in range(1, nrep):
        psum = psum + p_c[:, r * tq:(r + 1) * tq]
    hi = psum.astype(BF16)
    lo = (psum - hi.astype(F32)).astype(BF16)
    imp = _dot(ovt_ref[...], hi) + _dot(ovt_ref[...], lo)
    jb = _iota((LANES, tq), 0)
    cur = _div_pow2(qpos, NSA_SLC_BLOCK)
    forced = (jb == 0) | (jb == cur) | (jb == cur - 1)
    imp = jnp.where(forced, NSA_FORCE_SCORE, imp)
    imp = jnp.where(jb <= cur, imp, NEG)
    bias_ref[...] = jnp.where(_topk_rows(imp, topn) > 0.5, 0.0, NEG)

    def tile(t, m, l, diagonal):
        k0 = pl.multiple_of(t * tk, tk)
        bias = jnp.concatenate(
            [jnp.broadcast_to(bias_ref[pl.ds(t * per_tile + i, 1), :], (NSA_SLC_BLOCK, tq))
             for i in range(per_tile)], axis=0)
        if diagonal:
            bias = jnp.where(k0 + _iota((tk, 1), 0) <= qpos, bias, NEG)
        st = _dot(ks_ref[0, pl.ds(k0, tk), :], qst) + jnp.concatenate([bias] * nrep, axis=1)
        m_new = jnp.maximum(m, jnp.max(st, axis=0, keepdims=True))
        alpha = jnp.exp(m - m_new)
        p = jnp.exp(st - m_new)
        acc_ref[...] = alpha * acc_ref[...] + _dot(vst_ref[0, t], p.astype(BF16))
        return m_new, alpha * l + jnp.sum(p, axis=0, keepdims=True)

    acc_ref[...] = jnp.zeros_like(acc_ref)
    n_full = q0 // tk
    init = (jnp.full((1, nrep * tq), NEG, F32), jnp.zeros((1, nrep * tq), F32))
    m_s, l_s = lax.fori_loop(0, n_full, lambda t, ml: tile(t, ml[0], ml[1], False), init)
    m_s, l_s = tile(n_full, m_s, l_s, True)
    o_s = acc_ref[...] / l_s

    st0 = pl.multiple_of(jnp.maximum(q0 - NSA_WINDOW, 0), LANES)
    dlt = qpos_w - (st0 + _iota((nwin * LANES, 1), 0))
    s_w = jnp.where((dlt >= 0) & (dlt < NSA_WINDOW), _dot(kw_ref[0, pl.ds(st0, nwin * LANES), :], qst), NEG)
    p_w = jnp.exp(s_w - jnp.max(s_w, axis=0, keepdims=True))
    blk0 = st0 // LANES
    o_w = _dot(vwt_ref[0, blk0], p_w[0:LANES].astype(BF16))
    for i in range(1, nwin):
        o_w = o_w + _dot(vwt_ref[0, blk0 + i], p_w[i * LANES:(i + 1) * LANES].astype(BF16))
    o_w = o_w / jnp.sum(p_w, axis=0, keepdims=True)

    gts_ref[...] = gt_ref[0].T
    outs = []
    for r in range(nrep):
        base = (g * nrep + r) * 3
        cs = slice(r * tq, (r + 1) * tq)
        o_r = (gts_ref[pl.ds(base, 1), :] * o_c[:, cs] + gts_ref[pl.ds(base + 1, 1), :] * o_s[:, cs]
               + gts_ref[pl.ds(base + 2, 1), :] * o_w[:, cs])
        outs.append(jnp.where(g == r % 2, o_r, pltpu.roll(o_r, HEAD_DIM, 0)))
    pairs = [jnp.where(drow < HEAD_DIM, outs[2 * i], outs[2 * i + 1]).T for i in range(nrep // 2)]
    o_ref[0] = jnp.concatenate(pairs, axis=1).astype(o_ref.dtype)


def _nsa_overlap_t(s, ncp):
    nc = (s - NSA_CMP_LEN) // NSA_CMP_STRIDE + 1
    nsb = s // NSA_SLC_BLOCK
    cs = np.arange(nc) * NSA_CMP_STRIDE
    bs = np.arange(nsb) * NSA_SLC_BLOCK
    ov = (cs[None, :] < bs[:, None] + NSA_SLC_BLOCK) & (cs[None, :] + NSA_CMP_LEN > bs[:, None])
    out = np.zeros((LANES, ncp), np.float32)
    out[:nsb, :nc] = ov
    return jnp.asarray(out, BF16)


def _nsa(qkv, aux, kc, vc, b, s, tq=256, tk=512):
    qkv3 = qkv.reshape(b, s, -1)
    aux3 = aux.reshape(b, s, -1)
    ncp = kc.shape[1]
    nsb = s // NSA_SLC_BLOCK
    assert nsb <= LANES and s >= NSA_WINDOW + tq and s % tk == 0 and tk % tq == 0
    topn = min(NSA_SLC_TOPN, nsb)
    vct = vc.transpose(0, 2, 1)
    vst = _key_major(qkv3[..., G_NSA_VS * LANES:(G_NSA_VS + 1) * LANES], tk)[:, 0]
    vwt = _key_major(qkv3[..., G_NSA_VW * LANES:(G_NSA_VW + 1) * LANES], LANES)[:, 0]
    full = lambda grp: pl.BlockSpec((1, s, LANES), lambda i, g, c: (i, 0, grp))
    return pl.pallas_call(
        functools.partial(_nsa_kernel, topn=topn, tk=tk),
        grid=(b, 2, s // tq),
        in_specs=[
            pl.BlockSpec((1, tq, 2 * LANES), lambda i, g, c: (i, c, G_NSA_Q // 2 + g)),
            pl.BlockSpec((1, ncp, LANES), lambda i, g, c: (i, 0, 0)),
            pl.BlockSpec((1, LANES, ncp), lambda i, g, c: (i, 0, 0)),
            full(G_NSA_KS),
            pl.BlockSpec((1, s // tk, LANES, tk), lambda i, g, c: (i, 0, 0, 0)),
            full(G_NSA_KW),
            pl.BlockSpec((1, s // LANES, LANES, LANES), lambda i, g, c: (i, 0, 0, 0)),
            pl.BlockSpec((1, tq, LANES), lambda i, g, c: (i, c, 2)),
            pl.BlockSpec((LANES, ncp), lambda i, g, c: (0, 0)),
        ],
        out_specs=pl.BlockSpec((1, tq, 2 * LANES), lambda i, g, c: (i, c, g)),
        out_shape=jax.ShapeDtypeStruct((b, s, 4 * LANES), BF16),
        scratch_shapes=[pltpu.VMEM((LANES, tq), F32), pltpu.VMEM((LANES, tq), F32),
                        pltpu.VMEM((LANES, 4 * tq), F32)],
        compiler_params=_cparams("parallel", "parallel", "arbitrary"),
        name="nsa",
    )(qkv3, kc, vct, qkv3, vst, qkv3, vwt, aux3, _nsa_overlap_t(s, ncp))


def _sb_kernel(q_ref, k_ref, v_ref, o_ref):
    c = pl.program_id(2)
    tq = q_ref.shape[1]
    tk = tq
    lane = _iota((1, LANES), 1)
    q2 = q_ref[0]
    row = _iota((tq, tk), 0)
    kcol = _iota((tq, tk), 1)
    suffix = jnp.where(_iota((tk, tk), 0) >= _iota((tk, tk), 1), 1.0, 0.0).astype(BF16)

    outs = []
    for h in range(2):
        qh = jnp.where(_div_pow2(lane, HEAD_DIM) == h, q2, jnp.zeros_like(q2)) * jnp.asarray(HEAD_DIM ** -0.5, BF16)

        def cond(st):
            return (st[0] >= 0) & (st[1] < SB_SKIP_TAIL)

        def body(st):
            j, _, tail, acc = st
            k0 = pl.multiple_of(j * tk, tk)
            kj = k_ref[0, pl.ds(k0, tk), :]
            vj = v_ref[0, pl.ds(k0, tk), :]
            z = _dot_nt(qh, kj)
            causal = (j * tk + kcol) < (c * tq + row)
            sp = jnp.where(causal, jnp.maximum(z, 0.0) + jnp.log(1.0 + jnp.exp(-jnp.abs(z))), 0.0)
            incl = _dot_split(sp, suffix)
            a = jnp.where(causal, jnp.exp(z - incl - tail), 0.0)
            acc = acc + _dot(a.astype(BF16), vj)
            tail = tail + incl[:, 0:1]
            return (j - 1, jnp.min(tail), tail, acc)

        init = (c, jnp.float32(0.0), jnp.zeros((tq, 1), F32), jnp.zeros((tq, LANES), F32))
        outs.append(lax.while_loop(cond, body, init)[3])
    o_ref[0] = jnp.where(lane < HEAD_DIM, outs[0], outs[1]).astype(o_ref.dtype)


def _sb(qkv, b, s, tq=256):
    qkv3 = qkv.reshape(b, s, -1)
    gq, gk, gv = G_SB_Q, G_SB_K, G_SB_V
    return pl.pallas_call(
        _sb_kernel,
        grid=(b, 2, s // tq),
        in_specs=[
            pl.BlockSpec((1, tq, LANES), lambda i, p, c: (i, c, gq + p)),
            pl.BlockSpec((1, s, LANES), lambda i, p, c: (i, 0, gk + p)),
            pl.BlockSpec((1, s, LANES), lambda i, p, c: (i, 0, gv + p)),
        ],
        out_specs=pl.BlockSpec((1, tq, LANES), lambda i, p, c: (i, c, p)),
        out_shape=jax.ShapeDtypeStruct((b, s, 2 * LANES), BF16),
        compiler_params=_cparams("parallel", "parallel", "arbitrary"),
        name="stick_breaking",
    )(qkv3, qkv3, qkv3)


def _outproj_kernel(h_ref, om_ref, on_ref, os_ref, w_ref, g_ref, o_ref):
    dm, dn = om_ref.shape[1], on_ref.shape[1]
    mix = (_dot(om_ref[...], w_ref[0:dm, :]) + _dot(on_ref[...], w_ref[dm:dm + dn, :])
           + _dot(os_ref[...], w_ref[dm + dn:, :]))
    o_ref[...] = h_ref[...] + _rms(mix, g_ref[...])


def _outproj(h2, om, on, osb, w, g, tm=512):
    t, d = h2.shape
    row = lambda n: pl.BlockSpec((tm, n), lambda i: (i, 0))
    return pl.pallas_call(
        _outproj_kernel,
        grid=(t // tm,),
        in_specs=[row(d), row(om.shape[1]), row(on.shape[1]), row(osb.shape[1]),
                  pl.BlockSpec(w.shape, lambda i: (0, 0)), pl.BlockSpec((1, d), lambda i: (0, 0))],
        out_specs=row(d),
        out_shape=jax.ShapeDtypeStruct((t, d), F32),
        compiler_params=_cparams("parallel"),
        name="mixer_out",
    )(h2, om, on, osb, w, g)


def _memkv_kernel(m_ref, g_ref, w_ref, o_ref):
    o_ref[...] = _dot(_rms(m_ref[...], g_ref[...]).astype(BF16), w_ref[...]).astype(o_ref.dtype)


def _memkv(mem2, g, wkv, tm=256):
    t, d = mem2.shape
    n = wkv.shape[1]
    return pl.pallas_call(
        _memkv_kernel,
        grid=(t // tm,),
        in_specs=[pl.BlockSpec((tm, d), lambda i: (i, 0)), pl.BlockSpec((1, d), lambda i: (0, 0)),
                  pl.BlockSpec((d, n), lambda i: (0, 0))],
        out_specs=pl.BlockSpec((tm, n), lambda i: (i, 0)),
        out_shape=jax.ShapeDtypeStruct((t, n), BF16),
        compiler_params=_cparams("parallel"),
        name="mem_kv",
    )(mem2, g, wkv)


def _cross_kernel(h_ref, k_ref, v_ref, wq_ref, wo_ref, gpre_ref, gpost_ref, o_ref):
    h = h_ref[0]
    d = h.shape[1]
    hd = d // X_HEADS
    q = _dot(_rms(h, gpre_ref[...]).astype(BF16), wq_ref[...])
    outs = []
    for x in range(X_HEADS):
        cols = slice(x * hd, (x + 1) * hd)
        qh = (q[:, cols] * hd ** -0.5).astype(BF16)
        s = _dot_nt(qh, k_ref[0, :, cols])
        p = jnp.exp(s - jnp.max(s, axis=1, keepdims=True))
        p = p / jnp.sum(p, axis=1, keepdims=True)
        outs.append(_dot(p.astype(BF16), v_ref[0, :, cols]).astype(BF16))
    o = _dot(jnp.concatenate(outs, axis=1), wo_ref[...])
    o_ref[0] = h + _rms(o, gpost_ref[...])


def _cross(h3, kv3, wq, wo, gpre, gpost, tm=512):
    b, s, d = h3.shape
    m = kv3.shape[1]
    vec = pl.BlockSpec((1, d), lambda i, j: (0, 0))
    mat = pl.BlockSpec((d, d), lambda i, j: (0, 0))
    return pl.pallas_call(
        _cross_kernel,
        grid=(b, s // tm),
        in_specs=[pl.BlockSpec((1, tm, d), lambda i, j: (i, j, 0)),
                  pl.BlockSpec((1, m, d), lambda i, j: (i, 0, 0)),
                  pl.BlockSpec((1, m, d), lambda i, j: (i, 0, 1)),
                  mat, mat, vec, vec],
        out_specs=pl.BlockSpec((1, tm, d), lambda i, j: (i, j, 0)),
        out_shape=jax.ShapeDtypeStruct((b, s, d), F32),
        compiler_params=_cparams("parallel", "parallel"),
        name="cross_attn",
    )(h3, kv3, kv3, wq, wo, gpre, gpost)


def _ffn_kernel(h_ref, wup_ref, cw_ref, cb_ref, wdn_ref, gpre_ref, gpost_ref, o_ref, tail_ref, acc_ref,
                *, tc):
    j = pl.program_id(1)
    h = h_ref[0]
    tm = h.shape[0]
    dff = wdn_ref.shape[0]

    @pl.when(j == 0)
    def _():
        tail_ref[...] = jnp.zeros_like(tail_ref)

    xn = _rms(h, gpre_ref[...]).astype(BF16)
    row = _iota((tm, tc), 0)
    for c0 in range(0, dff, tc):
        cols = slice(c0, c0 + tc)
        a = _dot(xn, wup_ref[:, cols])
        gate = _dot(xn, wup_ref[:, dff + c0:dff + c0 + tc])
        prev1 = tail_ref[7:8, cols]
        prev2 = tail_ref[6:7, cols]
        a1 = jnp.where(row == 0, prev1, pltpu.roll(a, 1, 0))
        a2 = jnp.where(row == 0, prev2, jnp.where(row == 1, prev1, pltpu.roll(a, 2, 0)))
        tail_ref[:, cols] = a[tm - 8:tm, :]
        y = cw_ref[0:1, cols] * a2 + cw_ref[1:2, cols] * a1 + cw_ref[2:3, cols] * a + cb_ref[:, cols]
        part = _dot((_gelu_tanh(y) * gate).astype(BF16), wdn_ref[cols, :])
        if c0 == 0:
            acc_ref[...] = part
        else:
            acc_ref[...] += part
    o_ref[0] = h + _rms(acc_ref[...], gpost_ref[...])


def _ffn(h3, wup, cw, cb, wdn, gpre, gpost, tm=512, tc=256):
    b, s, d = h3.shape
    dff = wdn.shape[0]
    const = lambda shape: pl.BlockSpec(shape, lambda i, j: (0, 0))
    return pl.pallas_call(
        functools.partial(_ffn_kernel, tc=tc),
        grid=(b, s // tm),
        in_specs=[pl.BlockSpec((1, tm, d), lambda i, j: (i, j, 0)),
                  const(wup.shape), const(cw.shape), const(cb.shape), const(wdn.shape),
                  const((1, d)), const((1, d))],
        out_specs=pl.BlockSpec((1, tm, d), lambda i, j: (i, j, 0)),
        out_shape=jax.ShapeDtypeStruct((b, s, d), F32),
        scratch_shapes=[pltpu.VMEM((8, dff), F32), pltpu.VMEM((tm, d), F32)],
        compiler_params=_cparams("parallel", "arbitrary"),
        name="conv_ffn",
    )(h3, wup, cw, cb, wdn, gpre, gpost)


def _rope_tables(pos):
    inv_freq = ROPE_THETA ** (-jnp.arange(ROT_HALF, dtype=F32) / ROT_HALF)
    ang = pos.astype(F32)[..., None] * inv_freq
    cos, sin = jnp.cos(ang), jnp.sin(ang)
    rest = HEAD_DIM - 2 * ROT_HALF
    ones = jnp.ones(pos.shape + (rest,), F32)
    zeros = jnp.zeros(pos.shape + (rest,), F32)
    z8 = jnp.zeros_like(sin)
    two = lambda t: jnp.concatenate([t, t], axis=-1)
    return (two(jnp.concatenate([cos, cos, ones], -1)),
            two(jnp.concatenate([-sin, z8, zeros], -1)),
            two(jnp.concatenate([z8, sin, zeros], -1)))


def _pad_gate_cols(w_in):
    cut = G_GATE * LANES
    pad = jnp.zeros((w_in.shape[0], LANES - GATE_COLS), w_in.dtype)
    return jnp.concatenate([w_in[:, :cut + GATE_COLS], pad, w_in[:, cut + GATE_COLS:]], axis=1)


def kernel(x, mem, positions, norm_mix_pre, norm_mix_post, w_in, w_out, cmp_pe_k, cmp_pe_v, cmp_wk1, cmp_wk2, cmp_wv1, cmp_wv2, norm_x_pre, norm_x_post, norm_mem, x_wq, x_wk, x_wv, x_wo, norm_ffn_pre, norm_ffn_post, ffn_w_up, ffn_conv_w, ffn_conv_b, ffn_w_down):
    b, s, d = x.shape
    depth = w_in.shape[0]
    t = b * s
    rows = s // NSA_CMP_STRIDE
    c_tok, sa_tok, sb_tok = (tb.reshape(t, LANES) for tb in _rope_tables(positions))
    end_idx = jnp.minimum(jnp.arange(rows) * NSA_CMP_STRIDE + NSA_CMP_LEN - 1, s - 1)
    c_end, sa_end, sb_end = _rope_tables(positions[:, end_idx])
    vec = lambda g: g.reshape(1, -1)

    def blocks16(a):
        return a.reshape(b, s, 2, HEAD_DIM).transpose(0, 2, 1, 3).reshape(b, 2, rows, NSA_CMP_STRIDE * HEAD_DIM)

    h = x
    for l in range(depth):
        qkv, aux = _proj(h.reshape(t, d), vec(norm_mix_pre[l]), _pad_gate_cols(w_in[l]).astype(BF16),
                         c_tok, sa_tok, sb_tok)
        aux3 = aux.reshape(b, s, 3 * LANES)
        pe_k = cmp_pe_k[l].reshape(2, -1)
        pe_v = cmp_pe_v[l].reshape(2, -1)
        kc = _compress(blocks16(aux3[..., 0:LANES]), pe_k, cmp_wk1[l].astype(BF16), cmp_wk2[l].astype(BF16),
                       c_end, sa_end, sb_end, True)
        vc = _compress(blocks16(aux3[..., LANES:2 * LANES]), pe_v, cmp_wv1[l].astype(BF16),
                       cmp_wv2[l].astype(BF16), c_end, sa_end, sb_end, False)
        o_moba = _moba(qkv, b, s)
        o_nsa = _nsa(qkv, aux, kc, vc, b, s)
        o_sb = _sb(qkv, b, s)
        h = _outproj(h.reshape(t, d), o_moba.reshape(t, -1), o_nsa.reshape(t, -1), o_sb.reshape(t, -1),
                     w_out[l].astype(BF16), vec(norm_mix_post[l]))
        wkv = jnp.concatenate([x_wk[l], x_wv[l]], axis=1).astype(BF16)
        kv = _memkv(mem.reshape(-1, d), vec(norm_mem[l]), wkv).reshape(b, -1, 2 * d)
        h = _cross(h.reshape(b, s, d), kv, x_wq[l].astype(BF16), x_wo[l].astype(BF16),
                   vec(norm_x_pre[l]), vec(norm_x_post[l]))
        h = _ffn(h, ffn_w_up[l].astype(BF16), ffn_conv_w[l], vec(ffn_conv_b[l]), ffn_w_down[l].astype(BF16),
                 vec(norm_ffn_pre[l]), vec(norm_ffn_post[l]))
    return h
```

```python
import functools
import math

import jax
import jax.numpy as jnp
import numpy as np
from jax import lax
from jax.experimental import pallas as pl
from jax.experimental.pallas import tpu as pltpu

F32 = jnp.float32
BF16 = jnp.bfloat16

LANES = 128
HEAD_DIM = 64
ROT_HALF = 8
ROPE_THETA = 500000.0
MOBA_BLOCK = 256
MOBA_TOPK = 3
NSA_CMP_LEN = 32
NSA_CMP_STRIDE = 16
NSA_SLC_BLOCK = 64
NSA_SLC_TOPN = 16
NSA_WINDOW = 512
NSA_FORCE_SCORE = 1e4
X_HEADS = 4
EPS = 1e-6
NEG = -1e30
BELOW_NEG = -3e38
SB_SKIP_TAIL = 110.0
LOG2E = 1.4426950408889634
VMEM_LIMIT = 56 * 1024 * 1024

G_MOBA_Q, G_MOBA_K, G_MOBA_V = 0, 2, 4
G_NSA_Q = 6
G_NSA_KC, G_NSA_VC, G_NSA_KS, G_NSA_VS, G_NSA_KW, G_NSA_VW = 10, 11, 12, 13, 14, 15
G_GATE = 16
G_SB_Q, G_SB_K, G_SB_V = 17, 19, 21
N_GROUPS = 23
ROPE_GROUPS = (0, 1, 2, 3, 6, 7, 8, 9, 12, 14)
GATE_COLS = 24


def _cparams(*sem):
    return pltpu.CompilerParams(dimension_semantics=sem, vmem_limit_bytes=VMEM_LIMIT)


def _dot(a, b):
    return jnp.dot(a, b, preferred_element_type=F32)


def _dot_nt(a, b):
    return lax.dot_general(a, b, (((1,), (1,)), ((), ())), preferred_element_type=F32)


def _dot_split(a_f32, b_bf16):
    hi = a_f32.astype(BF16)
    lo = (a_f32 - hi.astype(F32)).astype(BF16)
    return _dot(hi, b_bf16) + _dot(lo, b_bf16)


def _rms(x, g):
    ms = jnp.mean(x * x, axis=-1, keepdims=True)
    return x * lax.rsqrt(ms + EPS) * g


def _gelu_tanh(x):
    c = math.sqrt(2.0 / math.pi)
    return 0.5 * x * (1.0 + jnp.tanh(c * (x + 0.044715 * (x * x * x))))


def _rope(y, c, sa, sb):
    return y * c + pltpu.roll(y, LANES - ROT_HALF, 1) * sa + pltpu.roll(y, ROT_HALF, 1) * sb


def _iota(shape, dim):
    return lax.broadcasted_iota(jnp.int32, shape, dim)


def _div_pow2(x, n):
    return lax.shift_right_logical(x, jnp.int32(int(math.log2(n))))


def _proj_kernel(x_ref, g_ref, w_ref, c_ref, sa_ref, sb_ref, qkv_ref, aux_ref):
    xn = _rms(x_ref[...], g_ref[...]).astype(BF16)
    c, sa, sb = c_ref[...], sa_ref[...], sb_ref[...]
    ncol = w_ref.shape[1]
    for j in range(0, ncol, 2 * LANES):
        wdt = min(2 * LANES, ncol - j)
        acc = _dot(xn, w_ref[:, j:j + wdt])
        for s in range(wdt // LANES):
            grp = j // LANES + s
            y = acc[:, s * LANES:(s + 1) * LANES]
            if grp in ROPE_GROUPS:
                y = _rope(y, c, sa, sb)
            if grp == G_NSA_KC:
                aux_ref[:, 0:LANES] = y
            elif grp == G_NSA_VC:
                aux_ref[:, LANES:2 * LANES] = y
            elif grp == G_GATE:
                aux_ref[:, 2 * LANES:3 * LANES] = 1.0 / (1.0 + jnp.exp(-y))
            qkv_ref[:, grp * LANES:(grp + 1) * LANES] = y.astype(BF16)


def _proj(h2, g, w, c, sa, sb, tm=512):
    t, d = h2.shape
    ncol = w.shape[1]
    return pl.pallas_call(
        _proj_kernel,
        grid=(t // tm,),
        in_specs=[
            pl.BlockSpec((tm, d), lambda i: (i, 0)),
            pl.BlockSpec((1, d), lambda i: (0, 0)),
            pl.BlockSpec((d, ncol), lambda i: (0, 0)),
            pl.BlockSpec((tm, LANES), lambda i: (i, 0)),
            pl.BlockSpec((tm, LANES), lambda i: (i, 0)),
            pl.BlockSpec((tm, LANES), lambda i: (i, 0)),
        ],
        out_specs=[
            pl.BlockSpec((tm, ncol), lambda i: (i, 0)),
            pl.BlockSpec((tm, 3 * LANES), lambda i: (i, 0)),
        ],
        out_shape=[
            jax.ShapeDtypeStruct((t, ncol), BF16),
            jax.ShapeDtypeStruct((t, 3 * LANES), F32),
        ],
        compiler_params=_cparams("parallel"),
        name="proj",
    )(h2, g, w, c, sa, sb)


def _topk_rows(score, k):
    rowf = _iota(score.shape, 0).astype(F32)
    for _ in range(k):
        m = jnp.max(score, axis=0, keepdims=True)
        idx = jnp.min(jnp.where(score == m, rowf, 1e9), axis=0, keepdims=True)
        score = jnp.where(rowf == idx, BELOW_NEG, score)
    return score == BELOW_NEG


def _moba_kernel(q_ref, k_ref, e_ref, vt_ref, o_ref, km_ref, acc_ref, sa_ref, sb_ref, *, nb, topk, per_tile):
    c = pl.program_id(2)
    tq = q_ref.shape[1]
    L = MOBA_BLOCK
    tk = per_tile * L

    @pl.when(c == 0)
    def _():
        km_ref[...] = jnp.zeros_like(km_ref)
        for j in range(nb):
            km = jnp.mean(k_ref[0, j * L:(j + 1) * L, :].astype(F32), axis=0, keepdims=True)
            hi = km.astype(BF16).astype(F32)
            km_ref[0, j:j + 1, :] = hi
            km_ref[1, j:j + 1, :] = km - hi

    drow = _iota((LANES, 1), 0)
    q2t = q_ref[0].astype(F32).T
    blk = _iota((LANES, tq), 0)
    kmh = km_ref[0].astype(BF16)
    kml = km_ref[1].astype(BF16)
    qts, biases = [], []
    for h in range(2):
        qt = jnp.where(_div_pow2(drow, HEAD_DIM) == h, q2t, 0.0)
        qb = qt.astype(BF16)
        gate = _dot(kmh, qb) + _dot(kml, qb)
        sel = _topk_rows(jnp.where(blk < c, gate, NEG), topk)
        biases.append(jnp.where(sel | (blk == c), 0.0, NEG).astype(BF16))
        qts.append((qt * (HEAD_DIM ** -0.5 * LOG2E)).astype(BF16))
    qa = jnp.concatenate([jnp.concatenate(qts, axis=1), jnp.concatenate(biases, axis=1)], axis=0)

    def scores(t):
        k0 = pl.multiple_of(t * tk, tk)
        return _dot(jnp.concatenate([k_ref[0, pl.ds(k0, tk), :], e_ref[pl.ds(k0, tk), :]], axis=1), qa)

    def tile(t, m, l, cur_ref, nxt_ref, last):
        st = cur_ref[...]
        if last:
            kpos = t * tk + _iota((tk, tq), 0)
            qpos = c * L + _iota((tk, tq), 1)
            tri = jnp.where(kpos <= qpos, 0.0, NEG)
            st = st + jnp.concatenate([tri, tri], axis=1)
        else:
            nxt_ref[...] = scores(t + 1)
        m_new = jnp.maximum(m, jnp.max(st, axis=0, keepdims=True))
        alpha = jnp.exp2(m - m_new)
        p = jnp.exp2(st - m_new)
        acc_ref[...] = alpha * acc_ref[...] + _dot(vt_ref[0, 0, t], p.astype(BF16))
        return m_new, alpha * l + jnp.sum(p, axis=0, keepdims=True)

    acc_ref[...] = jnp.zeros_like(acc_ref)
    sa_ref[...] = scores(0)
    n_full = c // per_tile

    def pair(u, ml):
        ml = tile(2 * u, ml[0], ml[1], sa_ref, sb_ref, False)
        return tile(2 * u + 1, ml[0], ml[1], sb_ref, sa_ref, False)

    init = (jnp.full((1, 2 * tq), NEG, F32), jnp.zeros((1, 2 * tq), F32))
    ml = lax.fori_loop(0, n_full // 2, pair, init)

    def odd_tail(ml):
        ml = tile(n_full - 1, ml[0], ml[1], sa_ref, sb_ref, False)
        return tile(n_full, ml[0], ml[1], sb_ref, sa_ref, True)

    m, l = lax.cond(n_full % 2 == 1, odd_tail, lambda ml: tile(n_full, ml[0], ml[1], sa_ref, sb_ref, True), ml)
    o = acc_ref[...] / l
    o_ref[0] = jnp.where(drow < HEAD_DIM, o[:, 0:tq], o[:, tq:2 * tq]).T.astype(o_ref.dtype)


def _key_major(a3, blk):
    b, s, c = a3.shape
    return a3.reshape(b, s // blk, blk, c // LANES, LANES).transpose(0, 3, 1, 4, 2)


def _block_onehot(s, blk):
    return jnp.asarray(np.arange(s)[:, None] // blk == np.arange(LANES)[None, :], BF16)


def _moba(qkv, b, s, per_tile=4):
    nb = s // MOBA_BLOCK
    assert nb % per_tile == 0 and nb <= LANES
    topk = max(1, min(MOBA_TOPK, nb - 1))
    tq = MOBA_BLOCK
    tk = per_tile * MOBA_BLOCK
    qkv3 = qkv.reshape(b, s, -1)
    gq, gk, gv = G_MOBA_Q, G_MOBA_K, G_MOBA_V
    vt = _key_major(qkv3[..., gv * LANES:(gv + 2) * LANES], tk)
    return pl.pallas_call(
        functools.partial(_moba_kernel, nb=nb, topk=topk, per_tile=per_tile),
        grid=(b, 2, s // tq),
        in_specs=[
            pl.BlockSpec((1, tq, LANES), lambda i, p, c: (i, c, gq + p)),
            pl.BlockSpec((1, s, LANES), lambda i, p, c: (i, 0, gk + p)),
            pl.BlockSpec((s, LANES), lambda i, p, c: (0, 0)),
            pl.BlockSpec((1, 1, s // tk, LANES, tk), lambda i, p, c: (i, p, 0, 0, 0)),
        ],
        out_specs=pl.BlockSpec((1, tq, LANES), lambda i, p, c: (i, c, p)),
        out_shape=jax.ShapeDtypeStruct((b, s, 2 * LANES), BF16),
        scratch_shapes=[pltpu.VMEM((2, LANES, LANES), F32), pltpu.VMEM((LANES, 2 * tq), F32),
                        pltpu.VMEM((tk, 2 * tq), F32), pltpu.VMEM((tk, 2 * tq), F32)],
        compiler_params=_cparams("parallel", "parallel", "arbitrary"),
        name="moba",
    )(qkv3, qkv3, _block_onehot(s, MOBA_BLOCK), vt)


def _cmp_kernel(x_ref, pe_ref, w1_ref, w2_ref, c_ref, sa_ref, sb_ref, o_ref, *, rope):
    rows = x_ref.shape[2]
    half = w1_ref.shape[0] // 2
    outs = []
    for g in range(2):
        x = x_ref[0, g]
        xa = (x + pe_ref[0:1, :]).astype(BF16)
        xb = (x + pe_ref[1:2, :]).astype(BF16)
        a = _dot(xa, w1_ref[0:half, :])
        bm = _dot(xb, w1_ref[half:2 * half, :])
        h1 = a + pltpu.roll(bm, rows - 1, 0)
        outs.append(_dot(_gelu_tanh(h1).astype(BF16), w2_ref[...]))
    y = jnp.concatenate(outs, axis=1)
    if rope:
        y = _rope(y, c_ref[0], sa_ref[0], sb_ref[0])
    o_ref[0] = y.astype(o_ref.dtype)


def _compress(x4, pe2, w1, w2, c, sa, sb, rope):
    b, g, rows, width = x4.shape
    hid = w1.shape[1]
    return pl.pallas_call(
        functools.partial(_cmp_kernel, rope=rope),
        grid=(b,),
        in_specs=[
            pl.BlockSpec((1, g, rows, width), lambda i: (i, 0, 0, 0)),
            pl.BlockSpec((2, width), lambda i: (0, 0)),
            pl.BlockSpec((2 * width, hid), lambda i: (0, 0)),
            pl.BlockSpec((hid, HEAD_DIM), lambda i: (0, 0)),
            pl.BlockSpec((1, rows, LANES), lambda i: (i, 0, 0)),
            pl.BlockSpec((1, rows, LANES), lambda i: (i, 0, 0)),
            pl.BlockSpec((1, rows, LANES), lambda i: (i, 0, 0)),
        ],
        out_specs=pl.BlockSpec((1, rows, LANES), lambda i: (i, 0, 0)),
        out_shape=jax.ShapeDtypeStruct((b, rows, LANES), BF16),
        compiler_params=_cparams("parallel"),
        name="nsa_compress",
    )(x4, pe2, w1, w2, c, sa, sb)


def _nsa_kernel(q_ref, kc_ref, vct_ref, ks_ref, e_ref, vst_ref, kw_ref, vwt_ref, gt_ref, ovt_ref, o_ref,
                gts_ref, acc_ref, ow_ref, sa_ref, sb_ref, *, topn, tk):
    g = pl.program_id(1)
    c = pl.program_id(2)
    tq = q_ref.shape[1]
    nrep = q_ref.shape[2] // HEAD_DIM
    ncp = kc_ref.shape[1]
    q0 = c * tq
    nwin = (NSA_WINDOW + tq) // LANES

    drow = _iota((LANES, 1), 0)
    in_g = _div_pow2(drow, HEAD_DIM) == g
    q2 = q_ref[0].astype(F32)
    cols = []
    for r in range(nrep):
        qpt = q2[:, (r // 2) * LANES:(r // 2 + 1) * LANES].T
        qpt = jnp.where(g == r % 2, qpt, pltpu.roll(qpt, HEAD_DIM, 0))
        cols.append(jnp.where(in_g, qpt, 0.0) * (HEAD_DIM ** -0.5 * LOG2E))
    qst = jnp.concatenate(cols, axis=1).astype(BF16)
    qpos = q0 + _iota((1, tq), 1)
    qpos_w = jnp.concatenate([qpos] * nrep, axis=1)

    st0 = pl.multiple_of(jnp.maximum(q0 - NSA_WINDOW, 0), LANES)
    dlt = qpos_w - (st0 + _iota((nwin * LANES, 1), 0))
    s_w = jnp.where((dlt >= 0) & (dlt < NSA_WINDOW), _dot(kw_ref[0, pl.ds(st0, nwin * LANES), :], qst), NEG)
    p_w = jnp.exp2(s_w - jnp.max(s_w, axis=0, keepdims=True))
    blk0 = st0 // LANES
    o_w = _dot(vwt_ref[0, blk0], p_w[0:LANES].astype(BF16))
    for i in range(1, nwin):
        o_w = o_w + _dot(vwt_ref[0, blk0 + i], p_w[i * LANES:(i + 1) * LANES].astype(BF16))
    ow_ref[...] = o_w / jnp.sum(p_w, axis=0, keepdims=True)

    s_c = _dot(kc_ref[0], qst)
    cmp_end = _iota((ncp, 1), 0) * NSA_CMP_STRIDE + (NSA_CMP_LEN - 1)
    mask_c = cmp_end <= qpos_w
    sm = jnp.where(mask_c, s_c, NEG)
    m = jnp.max(sm, axis=0, keepdims=True)
    e = jnp.where(mask_c, jnp.exp2(sm - m), 0.0)
    l = jnp.sum(e, axis=0, keepdims=True)
    p_c = e / jnp.where(l > 0.0, l, 1.0)
    o_c = _dot(vct_ref[0], p_c.astype(BF16))

    psum = p_c[:, 0:tq]
    for r in range(1, nrep):
        psum = psum + p_c[:, r * tq:(r + 1) * tq]
    hi = psum.astype(BF16)
    lo = (psum - hi.astype(F32)).astype(BF16)
    imp = _dot(ovt_ref[...], hi) + _dot(ovt_ref[...], lo)
    jb = _iota((LANES, tq), 0)
    cur = _div_pow2(qpos, NSA_SLC_BLOCK)
    forced = (jb == 0) | (jb == cur) | (jb == cur - 1)
    imp = jnp.where(forced, NSA_FORCE_SCORE, imp)
    imp = jnp.where(jb <= cur, imp, NEG)
    bias = jnp.where(_topk_rows(imp, topn), 0.0, NEG).astype(BF16)
    qa = jnp.concatenate([qst, jnp.concatenate([bias] * nrep, axis=1)], axis=0)

    def scores(t):
        k0 = pl.multiple_of(t * tk, tk)
        return _dot(jnp.concatenate([ks_ref[0, pl.ds(k0, tk), :], e_ref[pl.ds(k0, tk), :]], axis=1), qa)

    def tile(t, m, l, cur_ref, nxt_ref, diagonal):
        st = cur_ref[...]
        if diagonal:
            st = jnp.where(t * tk + _iota((tk, 1), 0) <= qpos_w, st, NEG)
        else:
            nxt_ref[...] = scores(t + 1)
        m_new = jnp.maximum(m, jnp.max(st, axis=0, keepdims=True))
        alpha = jnp.exp2(m - m_new)
        p = jnp.exp2(st - m_new)
        acc_ref[...] = alpha * acc_ref[...] + _dot(vst_ref[0, t], p.astype(BF16))
        return m_new, alpha * l + jnp.sum(p, axis=0, keepdims=True)

    acc_ref[...] = jnp.zeros_like(acc_ref)
    sa_ref[...] = scores(0)
    n_full = q0 // tk

    def pair(u, ml):
        ml = tile(2 * u, ml[0], ml[1], sa_ref, sb_ref, False)
        return tile(2 * u + 1, ml[0], ml[1], sb_ref, sa_ref, False)

    init = (jnp.full((1, nrep * tq), NEG, F32), jnp.zeros((1, nrep * tq), F32))
    ml = lax.fori_loop(0, n_full // 2, pair, init)

    def odd_tail(ml):
        ml = tile(n_full - 1, ml[0], ml[1], sa_ref, sb_ref, False)
        return tile(n_full, ml[0], ml[1], sb_ref, sa_ref, True)

    m_s, l_s = lax.cond(n_full % 2 == 1, odd_tail,
                        lambda ml: tile(n_full, ml[0], ml[1], sa_ref, sb_ref, True), ml)
    o_s = acc_ref[...] / l_s
    o_w = ow_ref[...]

    gts_ref[...] = gt_ref[0].T
    outs = []
    for r in range(nrep):
        base = (g * nrep + r) * 3
        cs = slice(r * tq, (r + 1) * tq)
        o_r = (gts_ref[pl.ds(base, 1), :] * o_c[:, cs] + gts_ref[pl.ds(base + 1, 1), :] * o_s[:, cs]
               + gts_ref[pl.ds(base + 2, 1), :] * o_w[:, cs])
        outs.append(jnp.where(g == r % 2, o_r, pltpu.roll(o_r, HEAD_DIM, 0)))
    pairs = [jnp.where(drow < HEAD_DIM, outs[2 * i], outs[2 * i + 1]).T for i in range(nrep // 2)]
    o_ref[0] = jnp.concatenate(pairs, axis=1).astype(o_ref.dtype)


def _nsa_overlap_t(s, ncp):
    nc = (s - NSA_CMP_LEN) // NSA_CMP_STRIDE + 1
    nsb = s // NSA_SLC_BLOCK
    cs = np.arange(nc) * NSA_CMP_STRIDE
    bs = np.arange(nsb) * NSA_SLC_BLOCK
    ov = (cs[None, :] < bs[:, None] + NSA_SLC_BLOCK) & (cs[None, :] + NSA_CMP_LEN > bs[:, None])
    out = np.zeros((LANES, ncp), np.float32)
    out[:nsb, :nc] = ov
    return jnp.asarray(out, BF16)


def _nsa(qkv, aux, kc, vc, b, s, tq=256, tk=512):
    qkv3 = qkv.reshape(b, s, -1)
    aux3 = aux.reshape(b, s, -1)
    ncp = kc.shape[1]
    nsb = s // NSA_SLC_BLOCK
    assert nsb <= LANES and s >= NSA_WINDOW + tq and s % tk == 0 and tk % tq == 0
    topn = min(NSA_SLC_TOPN, nsb)
    vct = vc.transpose(0, 2, 1)
    vst = _key_major(qkv3[..., G_NSA_VS * LANES:(G_NSA_VS + 1) * LANES], tk)[:, 0]
    vwt = _key_major(qkv3[..., G_NSA_VW * LANES:(G_NSA_VW + 1) * LANES], LANES)[:, 0]
    full = lambda grp: pl.BlockSpec((1, s, LANES), lambda i, g, c: (i, 0, grp))
    return pl.pallas_call(
        functools.partial(_nsa_kernel, topn=topn, tk=tk),
        grid=(b, 2, s // tq),
        in_specs=[
            pl.BlockSpec((1, tq, 2 * LANES), lambda i, g, c: (i, c, G_NSA_Q // 2 + g)),
            pl.BlockSpec((1, ncp, LANES), lambda i, g, c: (i, 0, 0)),
            pl.BlockSpec((1, LANES, ncp), lambda i, g, c: (i, 0, 0)),
            full(G_NSA_KS),
            pl.BlockSpec((s, LANES), lambda i, g, c: (0, 0)),
            pl.BlockSpec((1, s // tk, LANES, tk), lambda i, g, c: (i, 0, 0, 0)),
            full(G_NSA_KW),
            pl.BlockSpec((1, s // LANES, LANES, LANES), lambda i, g, c: (i, 0, 0, 0)),
            pl.BlockSpec((1, tq, LANES), lambda i, g, c: (i, c, 2)),
            pl.BlockSpec((LANES, ncp), lambda i, g, c: (0, 0)),
        ],
        out_specs=pl.BlockSpec((1, tq, 2 * LANES), lambda i, g, c: (i, c, g)),
        out_shape=jax.ShapeDtypeStruct((b, s, 4 * LANES), BF16),
        scratch_shapes=[pltpu.VMEM((LANES, tq), F32), pltpu.VMEM((LANES, 4 * tq), F32),
                        pltpu.VMEM((LANES, 4 * tq), F32), pltpu.VMEM((tk, 4 * tq), F32),
                        pltpu.VMEM((tk, 4 * tq), F32)],
        compiler_params=_cparams("parallel", "parallel", "arbitrary"),
        name="nsa",
    )(qkv3, kc, vct, qkv3, _block_onehot(s, NSA_SLC_BLOCK), vst, qkv3, vwt, aux3, _nsa_overlap_t(s, ncp))


def _sb_kernel(q_ref, k_ref, v_ref, u_ref, o_ref, *, tk):
    c = pl.program_id(2)
    tq = q_ref.shape[1]
    lane = _iota((1, LANES), 1)
    q2 = q_ref[0]
    qhs = [jnp.where(_div_pow2(lane, HEAD_DIM) == h, q2, jnp.zeros_like(q2)) * jnp.asarray(HEAD_DIM ** -0.5, BF16)
           for h in range(2)]
    qpos = c * tq + _iota((tq, tk), 0)
    kcol = _iota((tq, tk), 1)
    suffix = u_ref[...]

    def cond(st):
        return (st[0] > 0) & (st[1] < SB_SKIP_TAIL)

    def body(st):
        hi = st[0]
        lo = pl.multiple_of(jnp.maximum(hi - tk, 0), tq)
        kj = k_ref[0, pl.ds(lo, tk), :]
        vj = v_ref[0, pl.ds(lo, tk), :]
        kpos = lo + kcol
        valid = (kpos < qpos) & (kpos < hi)
        out = []
        for h in range(2):
            tail, acc = st[2 + 2 * h], st[3 + 2 * h]
            z = _dot_nt(qhs[h], kj)
            sp = jnp.where(valid, jnp.maximum(z, 0.0) + jnp.log(1.0 + jnp.exp(-jnp.abs(z))), 0.0)
            incl = _dot_split(sp, suffix)
            a = jnp.where(valid, jnp.exp(z - incl - tail), 0.0)
            out += [tail + incl[:, 0:1], acc + _dot(a.astype(BF16), vj)]
        return (lo, jnp.minimum(jnp.min(out[0]), jnp.min(out[2])), *out)

    zero_t = jnp.zeros((tq, 1), F32)
    zero_a = jnp.zeros((tq, LANES), F32)
    res = lax.while_loop(cond, body, ((c + 1) * tq, jnp.float32(0.0), zero_t, zero_a, zero_t, zero_a))
    o_ref[0] = jnp.where(lane < HEAD_DIM, res[3], res[5]).astype(o_ref.dtype)


def _sb(qkv, b, s, tq=256, tk=512):
    assert s >= tk and tk % tq == 0
    qkv3 = qkv.reshape(b, s, -1)
    gq, gk, gv = G_SB_Q, G_SB_K, G_SB_V
    suffix = jnp.asarray(np.arange(tk)[:, None] >= np.arange(tk)[None, :], BF16)
    return pl.pallas_call(
        functools.partial(_sb_kernel, tk=tk),
        grid=(b, 2, s // tq),
        in_specs=[
            pl.BlockSpec((1, tq, LANES), lambda i, p, c: (i, c, gq + p)),
            pl.BlockSpec((1, s, LANES), lambda i, p, c: (i, 0, gk + p)),
            pl.BlockSpec((1, s, LANES), lambda i, p, c: (i, 0, gv + p)),
            pl.BlockSpec((tk, tk), lambda i, p, c: (0, 0)),
        ],
        out_specs=pl.BlockSpec((1, tq, LANES), lambda i, p, c: (i, c, p)),
        out_shape=jax.ShapeDtypeStruct((b, s, 2 * LANES), BF16),
        compiler_params=_cparams("parallel", "parallel", "arbitrary"),
        name="stick_breaking",
    )(qkv3, qkv3, qkv3, suffix)


def _outproj_kernel(h_ref, om_ref, on_ref, os_ref, w_ref, g_ref, o_ref):
    dm, dn = om_ref.shape[1], on_ref.shape[1]
    mix = (_dot(om_ref[...], w_ref[0:dm, :]) + _dot(on_ref[...], w_ref[dm:dm + dn, :])
           + _dot(os_ref[...], w_ref[dm + dn:, :]))
    o_ref[...] = h_ref[...] + _rms(mix, g_ref[...])


def _outproj(h2, om, on, osb, w, g, tm=512):
    t, d = h2.shape
    row = lambda n: pl.BlockSpec((tm, n), lambda i: (i, 0))
    return pl.pallas_call(
        _outproj_kernel,
        grid=(t // tm,),
        in_specs=[row(d), row(om.shape[1]), row(on.shape[1]), row(osb.shape[1]),
                  pl.BlockSpec(w.shape, lambda i: (0, 0)), pl.BlockSpec((1, d), lambda i: (0, 0))],
        out_specs=row(d),
        out_shape=jax.ShapeDtypeStruct((t, d), F32),
        compiler_params=_cparams("parallel"),
        name="mixer_out",
    )(h2, om, on, osb, w, g)


def _memkv_kernel(m_ref, g_ref, w_ref, o_ref):
    o_ref[...] = _dot(_rms(m_ref[...], g_ref[...]).astype(BF16), w_ref[...]).astype(o_ref.dtype)


def _memkv(mem2, g, wkv, tm=256):
    t, d = mem2.shape
    n = wkv.shape[1]
    return pl.pallas_call(
        _memkv_kernel,
        grid=(t // tm,),
        in_specs=[pl.BlockSpec((tm, d), lambda i: (i, 0)), pl.BlockSpec((1, d), lambda i: (0, 0)),
                  pl.BlockSpec((d, n), lambda i: (0, 0))],
        out_specs=pl.BlockSpec((tm, n), lambda i: (i, 0)),
        out_shape=jax.ShapeDtypeStruct((t, n), BF16),
        compiler_params=_cparams("parallel"),
        name="mem_kv",
    )(mem2, g, wkv)


def _cross_kernel(h_ref, k_ref, v_ref, wq_ref, wo_ref, gpre_ref, gpost_ref, o_ref):
    h = h_ref[0]
    d = h.shape[1]
    hd = d // X_HEADS
    q = _dot(_rms(h, gpre_ref[...]).astype(BF16), wq_ref[...])
    outs = []
    for x in range(X_HEADS):
        cols = slice(x * hd, (x + 1) * hd)
        qh = (q[:, cols] * hd ** -0.5).astype(BF16)
        s = _dot_nt(qh, k_ref[0, :, cols])
        p = jnp.exp(s - jnp.max(s, axis=1, keepdims=True))
        p = p / jnp.sum(p, axis=1, keepdims=True)
        outs.append(_dot(p.astype(BF16), v_ref[0, :, cols]).astype(BF16))
    o = _dot(jnp.concatenate(outs, axis=1), wo_ref[...])
    o_ref[0] = h + _rms(o, gpost_ref[...])


def _cross(h3, kv3, wq, wo, gpre, gpost, tm=512):
    b, s, d = h3.shape
    m = kv3.shape[1]
    vec = pl.BlockSpec((1, d), lambda i, j: (0, 0))
    mat = pl.BlockSpec((d, d), lambda i, j: (0, 0))
    return pl.pallas_call(
        _cross_kernel,
        grid=(b, s // tm),
        in_specs=[pl.BlockSpec((1, tm, d), lambda i, j: (i, j, 0)),
                  pl.BlockSpec((1, m, d), lambda i, j: (i, 0, 0)),
                  pl.BlockSpec((1, m, d), lambda i, j: (i, 0, 1)),
                  mat, mat, vec, vec],
        out_specs=pl.BlockSpec((1, tm, d), lambda i, j: (i, j, 0)),
        out_shape=jax.ShapeDtypeStruct((b, s, d), F32),
        compiler_params=_cparams("parallel", "parallel"),
        name="cross_attn",
    )(h3, kv3, kv3, wq, wo, gpre, gpost)


def _ffn_kernel(h_ref, wup_ref, cw_ref, cb_ref, wdn_ref, gpre_ref, gpost_ref, o_ref, tail_ref, acc_ref,
                *, tc):
    j = pl.program_id(1)
    h = h_ref[0]
    tm = h.shape[0]
    dff = wdn_ref.shape[0]

    @pl.when(j == 0)
    def _():
        tail_ref[...] = jnp.zeros_like(tail_ref)

    xn = _rms(h, gpre_ref[...]).astype(BF16)
    row = _iota((tm, tc), 0)
    for c0 in range(0, dff, tc):
        cols = slice(c0, c0 + tc)
        a = _dot(xn, wup_ref[:, cols])
        gate = _dot(xn, wup_ref[:, dff + c0:dff + c0 + tc])
        prev1 = tail_ref[7:8, cols]
        prev2 = tail_ref[6:7, cols]
        a1 = jnp.where(row == 0, prev1, pltpu.roll(a, 1, 0))
        a2 = jnp.where(row == 0, prev2, jnp.where(row == 1, prev1, pltpu.roll(a, 2, 0)))
        tail_ref[:, cols] = a[tm - 8:tm, :]
        y = cw_ref[0:1, cols] * a2 + cw_ref[1:2, cols] * a1 + cw_ref[2:3, cols] * a + cb_ref[:, cols]
        part = _dot((_gelu_tanh(y) * gate).astype(BF16), wdn_ref[cols, :])
        if c0 == 0:
            acc_ref[...] = part
        else:
            acc_ref[...] += part
    o_ref[0] = h + _rms(acc_ref[...], gpost_ref[...])


def _ffn(h3, wup, cw, cb, wdn, gpre, gpost, tm=512, tc=256):
    b, s, d = h3.shape
    dff = wdn.shape[0]
    const = lambda shape: pl.BlockSpec(shape, lambda i, j: (0, 0))
    return pl.pallas_call(
        functools.partial(_ffn_kernel, tc=tc),
        grid=(b, s // tm),
        in_specs=[pl.BlockSpec((1, tm, d), lambda i, j: (i, j, 0)),
                  const(wup.shape), const(cw.shape), const(cb.shape), const(wdn.shape),
                  const((1, d)), const((1, d))],
        out_specs=pl.BlockSpec((1, tm, d), lambda i, j: (i, j, 0)),
        out_shape=jax.ShapeDtypeStruct((b, s, d), F32),
        scratch_shapes=[pltpu.VMEM((8, dff), F32), pltpu.VMEM((tm, d), F32)],
        compiler_params=_cparams("parallel", "arbitrary"),
        name="conv_ffn",
    )(h3, wup, cw, cb, wdn, gpre, gpost)


def _rope_tables(pos):
    inv_freq = ROPE_THETA ** (-jnp.arange(ROT_HALF, dtype=F32) / ROT_HALF)
    ang = pos.astype(F32)[..., None] * inv_freq
    cos, sin = jnp.cos(ang), jnp.sin(ang)
    rest = HEAD_DIM - 2 * ROT_HALF
    ones = jnp.ones(pos.shape + (rest,), F32)
    zeros = jnp.zeros(pos.shape + (rest,), F32)
    z8 = jnp.zeros_like(sin)
    two = lambda t: jnp.concatenate([t, t], axis=-1)
    return (two(jnp.concatenate([cos, cos, ones], -1)),
            two(jnp.concatenate([-sin, z8, zeros], -1)),
            two(jnp.concatenate([z8, sin, zeros], -1)))


def _pad_gate_cols(w_in):
    cut = G_GATE * LANES
    pad = jnp.zeros((w_in.shape[0], LANES - GATE_COLS), w_in.dtype)
    return jnp.concatenate([w_in[:, :cut + GATE_COLS], pad, w_in[:, cut + GATE_COLS:]], axis=1)


def kernel(x, mem, positions, norm_mix_pre, norm_mix_post, w_in, w_out, cmp_pe_k, cmp_pe_v, cmp_wk1, cmp_wk2, cmp_wv1, cmp_wv2, norm_x_pre, norm_x_post, norm_mem, x_wq, x_wk, x_wv, x_wo, norm_ffn_pre, norm_ffn_post, ffn_w_up, ffn_conv_w, ffn_conv_b, ffn_w_down):
    b, s, d = x.shape
    depth = w_in.shape[0]
    t = b * s
    rows = s // NSA_CMP_STRIDE
    c_tok, sa_tok, sb_tok = (tb.reshape(t, LANES) for tb in _rope_tables(positions))
    end_idx = jnp.minimum(jnp.arange(rows) * NSA_CMP_STRIDE + NSA_CMP_LEN - 1, s - 1)
    c_end, sa_end, sb_end = _rope_tables(positions[:, end_idx])
    vec = lambda g: g.reshape(1, -1)

    def blocks16(a):
        return a.reshape(b, s, 2, HEAD_DIM).transpose(0, 2, 1, 3).reshape(b, 2, rows, NSA_CMP_STRIDE * HEAD_DIM)

    h = x
    for l in range(depth):
        qkv, aux = _proj(h.reshape(t, d), vec(norm_mix_pre[l]), _pad_gate_cols(w_in[l]).astype(BF16),
                         c_tok, sa_tok, sb_tok)
        aux3 = aux.reshape(b, s, 3 * LANES)
        pe_k = cmp_pe_k[l].reshape(2, -1)
        pe_v = cmp_pe_v[l].reshape(2, -1)
        kc = _compress(blocks16(aux3[..., 0:LANES]), pe_k, cmp_wk1[l].astype(BF16), cmp_wk2[l].astype(BF16),
                       c_end, sa_end, sb_end, True)
        vc = _compress(blocks16(aux3[..., LANES:2 * LANES]), pe_v, cmp_wv1[l].astype(BF16),
                       cmp_wv2[l].astype(BF16), c_end, sa_end, sb_end, False)
        o_moba = _moba(qkv, b, s)
        o_nsa = _nsa(qkv, aux, kc, vc, b, s)
        o_sb = _sb(qkv, b, s)
        h = _outproj(h.reshape(t, d), o_moba.reshape(t, -1), o_nsa.reshape(t, -1), o_sb.reshape(t, -1),
                     w_out[l].astype(BF16), vec(norm_mix_post[l]))
        wkv = jnp.concatenate([x_wk[l], x_wv[l]], axis=1).astype(BF16)
        kv = _memkv(mem.reshape(-1, d), vec(norm_mem[l]), wkv).reshape(b, -1, 2 * d)
        h = _cross(h.reshape(b, s, d), kv, x_wq[l].astype(BF16), x_wo[l].astype(BF16),
                   vec(norm_x_pre[l]), vec(norm_x_post[l]))
        h = _ffn(h, ffn_w_up[l].astype(BF16), ffn_conv_w[l], vec(ffn_conv_b[l]), ffn_w_down[l].astype(BF16),
                 vec(norm_ffn_pre[l]), vec(norm_ffn_post[l]))
    return h
```

```python
import functools
import math

import jax
import jax.numpy as jnp
import numpy as np
from jax import lax
from jax.experimental import pallas as pl
from jax.experimental.pallas import tpu as pltpu

F32 = jnp.float32
BF16 = jnp.bfloat16

LANES = 128
HEAD_DIM = 64
ROT_HALF = 8
ROPE_THETA = 500000.0
MOBA_BLOCK = 256
MOBA_TOPK = 3
NSA_CMP_LEN = 32
NSA_CMP_STRIDE = 16
NSA_SLC_BLOCK = 64
NSA_SLC_TOPN = 16
NSA_WINDOW = 512
NSA_FORCE_SCORE = 1e4
X_HEADS = 4
EPS = 1e-6
NEG = -1e30
BELOW_NEG = -3e38
SB_SKIP_TAIL = 110.0
LOG2E = 1.4426950408889634
VMEM_LIMIT = 56 * 1024 * 1024

G_MOBA_Q, G_MOBA_K, G_MOBA_V = 0, 2, 4
G_NSA_Q = 6
G_NSA_KC, G_NSA_VC, G_NSA_KS, G_NSA_VS, G_NSA_KW, G_NSA_VW = 10, 11, 12, 13, 14, 15
G_GATE = 16
G_SB_Q, G_SB_K, G_SB_V = 17, 19, 21
N_GROUPS = 23
ROPE_GROUPS = (0, 1, 2, 3, 6, 7, 8, 9, 12, 14)
GATE_COLS = 24


def _cparams(*sem):
    return pltpu.CompilerParams(dimension_semantics=sem, vmem_limit_bytes=VMEM_LIMIT)


def _dot(a, b):
    return jnp.dot(a, b, preferred_element_type=F32)


def _dot_nt(a, b):
    return lax.dot_general(a, b, (((1,), (1,)), ((), ())), preferred_element_type=F32)


def _dot_split(a_f32, b_bf16):
    hi = a_f32.astype(BF16)
    lo = (a_f32 - hi.astype(F32)).astype(BF16)
    return _dot(hi, b_bf16) + _dot(lo, b_bf16)


def _rms(x, g):
    ms = jnp.mean(x * x, axis=-1, keepdims=True)
    return x * lax.rsqrt(ms + EPS) * g


def _gelu_tanh(x):
    c = math.sqrt(2.0 / math.pi)
    return 0.5 * x * (1.0 + jnp.tanh(c * (x + 0.044715 * (x * x * x))))


def _rope(y, tab):
    c, sa, sb = tab[:, 0:LANES], tab[:, LANES:2 * LANES], tab[:, 2 * LANES:3 * LANES]
    return y * c + pltpu.roll(y, LANES - ROT_HALF, 1) * sa + pltpu.roll(y, ROT_HALF, 1) * sb


def _iota(shape, dim):
    return lax.broadcasted_iota(jnp.int32, shape, dim)


def _div_pow2(x, n):
    return lax.shift_right_logical(x, jnp.int32(int(math.log2(n))))


def _proj_kernel(x_ref, g_ref, w_ref, tab_ref, qkv_ref, aux_ref):
    xn = _rms(x_ref[...], g_ref[...]).astype(BF16)
    tab = tab_ref[...]
    ncol = w_ref.shape[1]
    for j in range(0, ncol, 2 * LANES):
        wdt = min(2 * LANES, ncol - j)
        acc = _dot(xn, w_ref[:, j:j + wdt])
        for s in range(wdt // LANES):
            grp = j // LANES + s
            y = acc[:, s * LANES:(s + 1) * LANES]
            if grp in ROPE_GROUPS:
                y = _rope(y, tab)
            if grp == G_NSA_KC:
                aux_ref[:, 0:LANES] = y
            elif grp == G_NSA_VC:
                aux_ref[:, LANES:2 * LANES] = y
            elif grp == G_GATE:
                aux_ref[:, 2 * LANES:3 * LANES] = 1.0 / (1.0 + jnp.exp(-y))
            qkv_ref[:, grp * LANES:(grp + 1) * LANES] = y.astype(BF16)


def _proj(h2, g, w, tab, tm=512):
    t, d = h2.shape
    ncol = w.shape[1]
    return pl.pallas_call(
        _proj_kernel,
        grid=(t // tm,),
        in_specs=[
            pl.BlockSpec((tm, d), lambda i: (i, 0)),
            pl.BlockSpec((1, d), lambda i: (0, 0)),
            pl.BlockSpec((d, ncol), lambda i: (0, 0)),
            pl.BlockSpec((tm, 3 * LANES), lambda i: (i, 0)),
        ],
        out_specs=[
            pl.BlockSpec((tm, ncol), lambda i: (i, 0)),
            pl.BlockSpec((tm, 3 * LANES), lambda i: (i, 0)),
        ],
        out_shape=[
            jax.ShapeDtypeStruct((t, ncol), BF16),
            jax.ShapeDtypeStruct((t, 3 * LANES), F32),
        ],
        compiler_params=_cparams("parallel"),
        name="proj",
    )(h2, g, w, tab)


def _topk_rows(score, k):
    rowf = _iota(score.shape, 0).astype(F32)
    for _ in range(k):
        m = jnp.max(score, axis=0, keepdims=True)
        idx = jnp.min(jnp.where(score == m, rowf, 1e9), axis=0, keepdims=True)
        score = jnp.where(rowf == idx, BELOW_NEG, score)
    return score == BELOW_NEG


def _moba_kernel(q_ref, k_ref, e_ref, vt_ref, o_ref, km_ref, acc_ref, sa_ref, sb_ref, *, nb, topk, per_tile):
    c = pl.program_id(2)
    tq = q_ref.shape[1]
    L = MOBA_BLOCK
    tk = per_tile * L

    @pl.when(c == 0)
    def _():
        km_ref[...] = jnp.zeros_like(km_ref)
        for j in range(nb):
            km = jnp.mean(k_ref[0, j * L:(j + 1) * L, :].astype(F32), axis=0, keepdims=True)
            hi = km.astype(BF16).astype(F32)
            km_ref[0, j:j + 1, :] = hi
            km_ref[1, j:j + 1, :] = km - hi

    drow = _iota((LANES, 1), 0)
    q2t = q_ref[0].astype(F32).T
    blk = _iota((LANES, tq), 0)
    kmh = km_ref[0].astype(BF16)
    kml = km_ref[1].astype(BF16)
    qts, biases = [], []
    for h in range(2):
        qt = jnp.where(_div_pow2(drow, HEAD_DIM) == h, q2t, 0.0)
        qb = qt.astype(BF16)
        gate = _dot(kmh, qb) + _dot(kml, qb)
        sel = _topk_rows(jnp.where(blk < c, gate, NEG), topk)
        biases.append(jnp.where(sel | (blk == c), 0.0, NEG).astype(BF16))
        qts.append((qt * (HEAD_DIM ** -0.5 * LOG2E)).astype(BF16))
    qa = jnp.concatenate([jnp.concatenate(qts, axis=1), jnp.concatenate(biases, axis=1)], axis=0)

    def scores(t):
        k0 = pl.multiple_of(t * tk, tk)
        return _dot(jnp.concatenate([k_ref[0, pl.ds(k0, tk), :], e_ref[pl.ds(k0, tk), :]], axis=1), qa)

    def tile(t, m, l, cur_ref, nxt_ref, last):
        st = cur_ref[...]
        if last:
            kpos = t * tk + _iota((tk, tq), 0)
            qpos = c * L + _iota((tk, tq), 1)
            tri = jnp.where(kpos <= qpos, 0.0, NEG)
            st = st + jnp.concatenate([tri, tri], axis=1)
        else:
            nxt_ref[...] = scores(t + 1)
        m_new = jnp.maximum(m, jnp.max(st, axis=0, keepdims=True))
        alpha = jnp.exp2(m - m_new)
        p = jnp.exp2(st - m_new)
        acc_ref[...] = alpha * acc_ref[...] + _dot(vt_ref[0, 0, t], p.astype(BF16))
        return m_new, alpha * l + jnp.sum(p, axis=0, keepdims=True)

    acc_ref[...] = jnp.zeros_like(acc_ref)
    sa_ref[...] = scores(0)
    n_full = c // per_tile

    def pair(u, ml):
        ml = tile(2 * u, ml[0], ml[1], sa_ref, sb_ref, False)
        return tile(2 * u + 1, ml[0], ml[1], sb_ref, sa_ref, False)

    init = (jnp.full((1, 2 * tq), NEG, F32), jnp.zeros((1, 2 * tq), F32))
    ml = lax.fori_loop(0, n_full // 2, pair, init)

    def odd_tail(ml):
        ml = tile(n_full - 1, ml[0], ml[1], sa_ref, sb_ref, False)
        return tile(n_full, ml[0], ml[1], sb_ref, sa_ref, True)

    m, l = lax.cond(n_full % 2 == 1, odd_tail, lambda ml: tile(n_full, ml[0], ml[1], sa_ref, sb_ref, True), ml)
    o = acc_ref[...] / l
    o_ref[0] = jnp.where(drow < HEAD_DIM, o[:, 0:tq], o[:, tq:2 * tq]).T.astype(o_ref.dtype)


def _key_major(a3, blk):
    b, s, c = a3.shape
    return a3.reshape(b, s // blk, blk, c // LANES, LANES).transpose(0, 3, 1, 4, 2)


def _block_onehot(s, blk):
    return jnp.asarray(np.arange(s)[:, None] // blk == np.arange(LANES)[None, :], BF16)


def _moba(qkv, b, s, per_tile=4):
    nb = s // MOBA_BLOCK
    assert nb % per_tile == 0 and nb <= LANES
    topk = max(1, min(MOBA_TOPK, nb - 1))
    tq = MOBA_BLOCK
    tk = per_tile * MOBA_BLOCK
    qkv3 = qkv.reshape(b, s, -1)
    gq, gk, gv = G_MOBA_Q, G_MOBA_K, G_MOBA_V
    vt = _key_major(qkv3[..., gv * LANES:(gv + 2) * LANES], tk)
    return pl.pallas_call(
        functools.partial(_moba_kernel, nb=nb, topk=topk, per_tile=per_tile),
        grid=(b, 2, s // tq),
        in_specs=[
            pl.BlockSpec((1, tq, LANES), lambda i, p, c: (i, c, gq + p)),
            pl.BlockSpec((1, s, LANES), lambda i, p, c: (i, 0, gk + p)),
            pl.BlockSpec((s, LANES), lambda i, p, c: (0, 0)),
            pl.BlockSpec((1, 1, s // tk, LANES, tk), lambda i, p, c: (i, p, 0, 0, 0)),
        ],
        out_specs=pl.BlockSpec((1, tq, LANES), lambda i, p, c: (i, c, p)),
        out_shape=jax.ShapeDtypeStruct((b, s, 2 * LANES), BF16),
        scratch_shapes=[pltpu.VMEM((2, LANES, LANES), F32), pltpu.VMEM((LANES, 2 * tq), F32),
                        pltpu.VMEM((tk, 2 * tq), F32), pltpu.VMEM((tk, 2 * tq), F32)],
        compiler_params=_cparams("parallel", "parallel", "arbitrary"),
        name="moba",
    )(qkv3, qkv3, _block_onehot(s, MOBA_BLOCK), vt)


def _cmp_kernel(x_ref, pe_ref, w1_ref, w2_ref, tab_ref, o_ref, *, rope):
    rows = x_ref.shape[2]
    half = w1_ref.shape[0] // 2
    outs = []
    for g in range(2):
        x = x_ref[0, g]
        xa = (x + pe_ref[0:1, :]).astype(BF16)
        xb = (x + pe_ref[1:2, :]).astype(BF16)
        a = _dot(xa, w1_ref[0:half, :])
        bm = _dot(xb, w1_ref[half:2 * half, :])
        h1 = a + pltpu.roll(bm, rows - 1, 0)
        outs.append(_dot(_gelu_tanh(h1).astype(BF16), w2_ref[...]))
    y = jnp.concatenate(outs, axis=1)
    if rope:
        y = _rope(y, tab_ref[0])
    o_ref[0] = y.astype(o_ref.dtype)


def _compress(x4, pe2, w1, w2, tab, rope):
    b, g, rows, width = x4.shape
    hid = w1.shape[1]
    return pl.pallas_call(
        functools.partial(_cmp_kernel, rope=rope),
        grid=(b,),
        in_specs=[
            pl.BlockSpec((1, g, rows, width), lambda i: (i, 0, 0, 0)),
            pl.BlockSpec((2, width), lambda i: (0, 0)),
            pl.BlockSpec((2 * width, hid), lambda i: (0, 0)),
            pl.BlockSpec((hid, HEAD_DIM), lambda i: (0, 0)),
            pl.BlockSpec((1, rows, 3 * LANES), lambda i: (i, 0, 0)),
        ],
        out_specs=pl.BlockSpec((1, rows, LANES), lambda i: (i, 0, 0)),
        out_shape=jax.ShapeDtypeStruct((b, rows, LANES), BF16),
        compiler_params=_cparams("parallel"),
        name="nsa_compress",
    )(x4, pe2, w1, w2, tab)


def _nsa_kernel(q_ref, kc_ref, vct_ref, ks_ref, e_ref, vst_ref, kw_ref, vwt_ref, gt_ref, ovt_ref, o_ref,
                gts_ref, acc_ref, ow_ref, sa_ref, sb_ref, *, topn, tk):
    g = pl.program_id(1)
    c = pl.program_id(2)
    tq = q_ref.shape[1]
    nrep = q_ref.shape[2] // HEAD_DIM
    ncp = kc_ref.shape[1]
    q0 = c * tq
    nwin = (NSA_WINDOW + tq) // LANES

    drow = _iota((LANES, 1), 0)
    in_g = _div_pow2(drow, HEAD_DIM) == g
    q2 = q_ref[0].astype(F32)
    cols = []
    for r in range(nrep):
        qpt = q2[:, (r // 2) * LANES:(r // 2 + 1) * LANES].T
        qpt = jnp.where(g == r % 2, qpt, pltpu.roll(qpt, HEAD_DIM, 0))
        cols.append(jnp.where(in_g, qpt, 0.0) * (HEAD_DIM ** -0.5 * LOG2E))
    qst = jnp.concatenate(cols, axis=1).astype(BF16)
    qpos = q0 + _iota((1, tq), 1)
    qpos_w = jnp.concatenate([qpos] * nrep, axis=1)

    s_c = _dot(kc_ref[0], qst)
    cmp_end = _iota((ncp, 1), 0) * NSA_CMP_STRIDE + (NSA_CMP_LEN - 1)
    mask_c = cmp_end <= qpos_w
    sm = jnp.where(mask_c, s_c, NEG)
    m = jnp.max(sm, axis=0, keepdims=True)
    e = jnp.where(mask_c, jnp.exp2(sm - m), 0.0)
    l = jnp.sum(e, axis=0, keepdims=True)
    p_c = e / jnp.where(l > 0.0, l, 1.0)
    o_c = _dot(vct_ref[0], p_c.astype(BF16))

    psum = p_c[:, 0:tq]
    for r in range(1, nrep):
        psum = psum + p_c[:, r * tq:(r + 1) * tq]
    hi = psum.astype(BF16)
    lo = (psum - hi.astype(F32)).astype(BF16)
    imp = _dot(ovt_ref[...], hi) + _dot(ovt_ref[...], lo)
    jb = _iota((LANES, tq), 0)
    cur = _div_pow2(qpos, NSA_SLC_BLOCK)
    forced = (jb == 0) | (jb == cur) | (jb == cur - 1)
    imp = jnp.where(forced, NSA_FORCE_SCORE, imp)
    imp = jnp.where(jb <= cur, imp, NEG)
    bias = jnp.where(_topk_rows(imp, topn), 0.0, NEG).astype(BF16)

    st0 = pl.multiple_of(jnp.maximum(q0 - NSA_WINDOW, 0), LANES)
    dlt = qpos_w - (st0 + _iota((nwin * LANES, 1), 0))
    s_w = jnp.where((dlt >= 0) & (dlt < NSA_WINDOW), _dot(kw_ref[0, pl.ds(st0, nwin * LANES), :], qst), NEG)
    p_w = jnp.exp2(s_w - jnp.max(s_w, axis=0, keepdims=True))
    blk0 = st0 // LANES
    o_w = _dot(vwt_ref[0, blk0], p_w[0:LANES].astype(BF16))
    for i in range(1, nwin):
        o_w = o_w + _dot(vwt_ref[0, blk0 + i], p_w[i * LANES:(i + 1) * LANES].astype(BF16))
    ow_ref[...] = o_w / jnp.sum(p_w, axis=0, keepdims=True)

    qa = jnp.concatenate([qst, jnp.concatenate([bias] * nrep, axis=1)], axis=0)

    def scores(t):
        k0 = pl.multiple_of(t * tk, tk)
        return _dot(jnp.concatenate([ks_ref[0, pl.ds(k0, tk), :], e_ref[pl.ds(k0, tk), :]], axis=1), qa)

    def tile(t, m, l, cur_ref, nxt_ref, diagonal):
        st = cur_ref[...]
        if diagonal:
            st = jnp.where(t * tk + _iota((tk, 1), 0) <= qpos_w, st, NEG)
        else:
            nxt_ref[...] = scores(t + 1)
        m_new = jnp.maximum(m, jnp.max(st, axis=0, keepdims=True))
        alpha = jnp.exp2(m - m_new)
        p = jnp.exp2(st - m_new)
        acc_ref[...] = alpha * acc_ref[...] + _dot(vst_ref[0, t], p.astype(BF16))
        return m_new, alpha * l + jnp.sum(p, axis=0, keepdims=True)

    acc_ref[...] = jnp.zeros_like(acc_ref)
    sa_ref[...] = scores(0)
    n_full = q0 // tk

    def pair(u, ml):
        ml = tile(2 * u, ml[0], ml[1], sa_ref, sb_ref, False)
        return tile(2 * u + 1, ml[0], ml[1], sb_ref, sa_ref, False)

    init = (jnp.full((1, nrep * tq), NEG, F32), jnp.zeros((1, nrep * tq), F32))
    ml = lax.fori_loop(0, n_full // 2, pair, init)

    def odd_tail(ml):
        ml = tile(n_full - 1, ml[0], ml[1], sa_ref, sb_ref, False)
        return tile(n_full, ml[0], ml[1], sb_ref, sa_ref, True)

    m_s, l_s = lax.cond(n_full % 2 == 1, odd_tail,
                        lambda ml: tile(n_full, ml[0], ml[1], sa_ref, sb_ref, True), ml)
    o_s = acc_ref[...] / l_s
    o_w = ow_ref[...]

    gts_ref[...] = gt_ref[0].T
    outs = []
    for r in range(nrep):
        base = (g * nrep + r) * 3
        cs = slice(r * tq, (r + 1) * tq)
        o_r = (gts_ref[pl.ds(base, 1), :] * o_c[:, cs] + gts_ref[pl.ds(base + 1, 1), :] * o_s[:, cs]
               + gts_ref[pl.ds(base + 2, 1), :] * o_w[:, cs])
        outs.append(jnp.where(g == r % 2, o_r, pltpu.roll(o_r, HEAD_DIM, 0)))
    pairs = [jnp.where(drow < HEAD_DIM, outs[2 * i], outs[2 * i + 1]).T for i in range(nrep // 2)]
    o_ref[0] = jnp.concatenate(pairs, axis=1).astype(o_ref.dtype)


def _nsa_overlap_t(s, ncp):
    nc = (s - NSA_CMP_LEN) // NSA_CMP_STRIDE + 1
    nsb = s // NSA_SLC_BLOCK
    cs = np.arange(nc) * NSA_CMP_STRIDE
    bs = np.arange(nsb) * NSA_SLC_BLOCK
    ov = (cs[None, :] < bs[:, None] + NSA_SLC_BLOCK) & (cs[None, :] + NSA_CMP_LEN > bs[:, None])
    out = np.zeros((LANES, ncp), np.float32)
    out[:nsb, :nc] = ov
    return jnp.asarray(out, BF16)


def _nsa(qkv, aux, kc, vc, b, s, tq=256, tk=512):
    qkv3 = qkv.reshape(b, s, -1)
    aux3 = aux.reshape(b, s, -1)
    ncp = kc.shape[1]
    nsb = s // NSA_SLC_BLOCK
    assert nsb <= LANES and s >= NSA_WINDOW + tq and s % tk == 0 and tk % tq == 0
    topn = min(NSA_SLC_TOPN, nsb)
    vct = vc.transpose(0, 2, 1)
    vst = _key_major(qkv3[..., G_NSA_VS * LANES:(G_NSA_VS + 1) * LANES], tk)[:, 0]
    vwt = _key_major(qkv3[..., G_NSA_VW * LANES:(G_NSA_VW + 1) * LANES], LANES)[:, 0]
    full = lambda grp: pl.BlockSpec((1, s, LANES), lambda i, g, c: (i, 0, grp))
    return pl.pallas_call(
        functools.partial(_nsa_kernel, topn=topn, tk=tk),
        grid=(b, 2, s // tq),
        in_specs=[
            pl.BlockSpec((1, tq, 2 * LANES), lambda i, g, c: (i, c, G_NSA_Q // 2 + g)),
            pl.BlockSpec((1, ncp, LANES), lambda i, g, c: (i, 0, 0)),
            pl.BlockSpec((1, LANES, ncp), lambda i, g, c: (i, 0, 0)),
            full(G_NSA_KS),
            pl.BlockSpec((s, LANES), lambda i, g, c: (0, 0)),
            pl.BlockSpec((1, s // tk, LANES, tk), lambda i, g, c: (i, 0, 0, 0)),
            full(G_NSA_KW),
            pl.BlockSpec((1, s // LANES, LANES, LANES), lambda i, g, c: (i, 0, 0, 0)),
            pl.BlockSpec((1, tq, LANES), lambda i, g, c: (i, c, 2)),
            pl.BlockSpec((LANES, ncp), lambda i, g, c: (0, 0)),
        ],
        out_specs=pl.BlockSpec((1, tq, 2 * LANES), lambda i, g, c: (i, c, g)),
        out_shape=jax.ShapeDtypeStruct((b, s, 4 * LANES), BF16),
        scratch_shapes=[pltpu.VMEM((LANES, tq), F32), pltpu.VMEM((LANES, 4 * tq), F32),
                        pltpu.VMEM((LANES, 4 * tq), F32), pltpu.VMEM((tk, 4 * tq), F32),
                        pltpu.VMEM((tk, 4 * tq), F32)],
        compiler_params=_cparams("parallel", "parallel", "arbitrary"),
        name="nsa",
    )(qkv3, kc, vct, qkv3, _block_onehot(s, NSA_SLC_BLOCK), vst, qkv3, vwt, aux3, _nsa_overlap_t(s, ncp))


def _sb_kernel(q_ref, k_ref, v_ref, u_ref, o_ref, *, tk):
    c = pl.program_id(2)
    tq = q_ref.shape[1]
    lane = _iota((1, LANES), 1)
    q2 = q_ref[0]
    qhs = [jnp.where(_div_pow2(lane, HEAD_DIM) == h, q2, jnp.zeros_like(q2)) * jnp.asarray(HEAD_DIM ** -0.5, BF16)
           for h in range(2)]
    qpos = c * tq + _iota((tq, tk), 0)
    kcol = _iota((tq, tk), 1)
    suffix = u_ref[...]

    def cond(st):
        return (st[0] > 0) & (st[1] < SB_SKIP_TAIL)

    def body(st):
        hi = st[0]
        lo = pl.multiple_of(jnp.maximum(hi - tk, 0), tq)
        kj = k_ref[0, pl.ds(lo, tk), :]
        vj = v_ref[0, pl.ds(lo, tk), :]
        kpos = lo + kcol
        valid = (kpos < qpos) & (kpos < hi)
        zs = [_dot_nt(qhs[h], kj) for h in range(2)]
        sps = [jnp.where(valid, jnp.maximum(z, 0.0) + jnp.log(1.0 + jnp.exp(-jnp.abs(z))), 0.0) for z in zs]
        incls = [_dot_split(sp, suffix) for sp in sps]
        ws = [jnp.where(valid, jnp.exp(zs[h] - incls[h] - st[2 + 2 * h]), 0.0) for h in range(2)]
        out = []
        for h in range(2):
            out += [st[2 + 2 * h] + incls[h][:, 0:1], st[3 + 2 * h] + _dot(ws[h].astype(BF16), vj)]
        return (lo, jnp.minimum(jnp.min(out[0]), jnp.min(out[2])), *out)

    zero_t = jnp.zeros((tq, 1), F32)
    zero_a = jnp.zeros((tq, LANES), F32)
    res = lax.while_loop(cond, body, ((c + 1) * tq, jnp.float32(0.0), zero_t, zero_a, zero_t, zero_a))
    o_ref[0] = jnp.where(lane < HEAD_DIM, res[3], res[5]).astype(o_ref.dtype)


def _sb(qkv, b, s, tq=256, tk=512):
    assert s >= tk and tk % tq == 0
    qkv3 = qkv.reshape(b, s, -1)
    gq, gk, gv = G_SB_Q, G_SB_K, G_SB_V
    suffix = jnp.asarray(np.arange(tk)[:, None] >= np.arange(tk)[None, :], BF16)
    return pl.pallas_call(
        functools.partial(_sb_kernel, tk=tk),
        grid=(b, 2, s // tq),
        in_specs=[
            pl.BlockSpec((1, tq, LANES), lambda i, p, c: (i, c, gq + p)),
            pl.BlockSpec((1, s, LANES), lambda i, p, c: (i, 0, gk + p)),
            pl.BlockSpec((1, s, LANES), lambda i, p, c: (i, 0, gv + p)),
            pl.BlockSpec((tk, tk), lambda i, p, c: (0, 0)),
        ],
        out_specs=pl.BlockSpec((1, tq, LANES), lambda i, p, c: (i, c, p)),
        out_shape=jax.ShapeDtypeStruct((b, s, 2 * LANES), BF16),
        compiler_params=_cparams("parallel", "parallel", "arbitrary"),
        name="stick_breaking",
    )(qkv3, qkv3, qkv3, suffix)


def _outproj_kernel(h_ref, om_ref, on_ref, os_ref, w_ref, g_ref, o_ref):
    dm, dn = om_ref.shape[1], on_ref.shape[1]
    mix = (_dot(om_ref[...], w_ref[0:dm, :]) + _dot(on_ref[...], w_ref[dm:dm + dn, :])
           + _dot(os_ref[...], w_ref[dm + dn:, :]))
    o_ref[...] = h_ref[...] + _rms(mix, g_ref[...])


def _outproj(h2, om, on, osb, w, g, tm=1024):
    t, d = h2.shape
    row = lambda n: pl.BlockSpec((tm, n), lambda i: (i, 0))
    return pl.pallas_call(
        _outproj_kernel,
        grid=(t // tm,),
        in_specs=[row(d), row(om.shape[1]), row(on.shape[1]), row(osb.shape[1]),
                  pl.BlockSpec(w.shape, lambda i: (0, 0)), pl.BlockSpec((1, d), lambda i: (0, 0))],
        out_specs=row(d),
        out_shape=jax.ShapeDtypeStruct((t, d), F32),
        compiler_params=_cparams("parallel"),
        name="mixer_out",
    )(h2, om, on, osb, w, g)


def _memkv_kernel(m_ref, g_ref, w_ref, o_ref):
    o_ref[...] = _dot(_rms(m_ref[...], g_ref[...]).astype(BF16), w_ref[...]).astype(o_ref.dtype)


def _memkv(mem2, g, wkv, tm=256):
    t, d = mem2.shape
    n = wkv.shape[1]
    return pl.pallas_call(
        _memkv_kernel,
        grid=(t // tm,),
        in_specs=[pl.BlockSpec((tm, d), lambda i: (i, 0)), pl.BlockSpec((1, d), lambda i: (0, 0)),
                  pl.BlockSpec((d, n), lambda i: (0, 0))],
        out_specs=pl.BlockSpec((tm, n), lambda i: (i, 0)),
        out_shape=jax.ShapeDtypeStruct((t, n), BF16),
        compiler_params=_cparams("parallel"),
        name="mem_kv",
    )(mem2, g, wkv)


def _cross_kernel(h_ref, k_ref, v_ref, wq_ref, wo_ref, gpre_ref, gpost_ref, o_ref):
    h = h_ref[0]
    d = h.shape[1]
    hd = d // X_HEADS
    q = _dot(_rms(h, gpre_ref[...]).astype(BF16), wq_ref[...])
    cols = [slice(x * hd, (x + 1) * hd) for x in range(X_HEADS)]
    ss = [_dot_nt((q[:, c] * hd ** -0.5).astype(BF16), k_ref[0, :, c]) for c in cols]
    ps = []
    for s in ss:
        p = jnp.exp(s - jnp.max(s, axis=1, keepdims=True))
        ps.append((p / jnp.sum(p, axis=1, keepdims=True)).astype(BF16))
    outs = [_dot(p, v_ref[0, :, c]).astype(BF16) for p, c in zip(ps, cols)]
    o = _dot(jnp.concatenate(outs, axis=1), wo_ref[...])
    o_ref[0] = h + _rms(o, gpost_ref[...])


def _cross(h3, kv3, wq, wo, gpre, gpost, tm=512):
    b, s, d = h3.shape
    m = kv3.shape[1]
    vec = pl.BlockSpec((1, d), lambda i, j: (0, 0))
    mat = pl.BlockSpec((d, d), lambda i, j: (0, 0))
    return pl.pallas_call(
        _cross_kernel,
        grid=(b, s // tm),
        in_specs=[pl.BlockSpec((1, tm, d), lambda i, j: (i, j, 0)),
                  pl.BlockSpec((1, m, d), lambda i, j: (i, 0, 0)),
                  pl.BlockSpec((1, m, d), lambda i, j: (i, 0, 1)),
                  mat, mat, vec, vec],
        out_specs=pl.BlockSpec((1, tm, d), lambda i, j: (i, j, 0)),
        out_shape=jax.ShapeDtypeStruct((b, s, d), F32),
        compiler_params=_cparams("parallel", "parallel"),
        name="cross_attn",
    )(h3, kv3, kv3, wq, wo, gpre, gpost)


def _ffn_kernel(h_ref, wup_ref, cw_ref, cb_ref, wdn_ref, gpre_ref, gpost_ref, o_ref, tail_ref, acc_ref,
                *, tc):
    j = pl.program_id(1)
    h = h_ref[0]
    tm = h.shape[0]
    dff = wdn_ref.shape[0]

    @pl.when(j == 0)
    def _():
        tail_ref[...] = jnp.zeros_like(tail_ref)

    xn = _rms(h, gpre_ref[...]).astype(BF16)
    row = _iota((tm, tc), 0)

    def up(c0):
        return _dot(xn, wup_ref[:, c0:c0 + tc]), _dot(xn, wup_ref[:, dff + c0:dff + c0 + tc])

    nxt = up(0)
    for c0 in range(0, dff, tc):
        cols = slice(c0, c0 + tc)
        a, gate = nxt
        if c0 + tc < dff:
            nxt = up(c0 + tc)
        prev1 = tail_ref[7:8, cols]
        prev2 = tail_ref[6:7, cols]
        a1 = jnp.where(row == 0, prev1, pltpu.roll(a, 1, 0))
        a2 = jnp.where(row == 0, prev2, jnp.where(row == 1, prev1, pltpu.roll(a, 2, 0)))
        tail_ref[:, cols] = a[tm - 8:tm, :]
        y = cw_ref[0:1, cols] * a2 + cw_ref[1:2, cols] * a1 + cw_ref[2:3, cols] * a + cb_ref[:, cols]
        part = _dot((_gelu_tanh(y) * gate).astype(BF16), wdn_ref[cols, :])
        if c0 == 0:
            acc_ref[...] = part
        else:
            acc_ref[...] += part
    o_ref[0] = h + _rms(acc_ref[...], gpost_ref[...])


def _ffn(h3, wup, cw, cb, wdn, gpre, gpost, tm=256, tc=256):
    b, s, d = h3.shape
    dff = wdn.shape[0]
    const = lambda shape: pl.BlockSpec(shape, lambda i, j: (0, 0))
    return pl.pallas_call(
        functools.partial(_ffn_kernel, tc=tc),
        grid=(b, s // tm),
        in_specs=[pl.BlockSpec((1, tm, d), lambda i, j: (i, j, 0)),
                  const(wup.shape), const(cw.shape), const(cb.shape), const(wdn.shape),
                  const((1, d)), const((1, d))],
        out_specs=pl.BlockSpec((1, tm, d), lambda i, j: (i, j, 0)),
        out_shape=jax.ShapeDtypeStruct((b, s, d), F32),
        scratch_shapes=[pltpu.VMEM((8, dff), F32), pltpu.VMEM((tm, d), F32)],
        compiler_params=_cparams("parallel", "arbitrary"),
        name="conv_ffn",
    )(h3, wup, cw, cb, wdn, gpre, gpost)


def _rope_tables(pos):
    inv_freq = ROPE_THETA ** (-jnp.arange(ROT_HALF, dtype=F32) / ROT_HALF)
    ang = pos.astype(F32)[..., None] * inv_freq
    cs = jnp.concatenate([jnp.cos(ang), jnp.sin(ang)], axis=-1)
    dim = np.arange(LANES) % HEAD_DIM
    spread = np.zeros((2 * ROT_HALF, 3 * LANES), np.float32)
    ones = np.zeros((3 * LANES,), np.float32)
    for lane in range(LANES):
        d = dim[lane]
        if d < ROT_HALF:
            spread[d, lane] = 1.0
            spread[ROT_HALF + d, LANES + lane] = -1.0
        elif d < 2 * ROT_HALF:
            spread[d - ROT_HALF, lane] = 1.0
            spread[d, 2 * LANES + lane] = 1.0
        else:
            ones[lane] = 1.0
    return jnp.dot(cs, jnp.asarray(spread), precision=lax.Precision.HIGHEST) + jnp.asarray(ones)


def _pad_gate_cols(w_in):
    cut = G_GATE * LANES
    pad = jnp.zeros((w_in.shape[0], LANES - GATE_COLS), w_in.dtype)
    return jnp.concatenate([w_in[:, :cut + GATE_COLS], pad, w_in[:, cut + GATE_COLS:]], axis=1)


def kernel(x, mem, positions, norm_mix_pre, norm_mix_post, w_in, w_out, cmp_pe_k, cmp_pe_v, cmp_wk1, cmp_wk2, cmp_wv1, cmp_wv2, norm_x_pre, norm_x_post, norm_mem, x_wq, x_wk, x_wv, x_wo, norm_ffn_pre, norm_ffn_post, ffn_w_up, ffn_conv_w, ffn_conv_b, ffn_w_down):
    b, s, d = x.shape
    depth = w_in.shape[0]
    t = b * s
    rows = s // NSA_CMP_STRIDE
    tab_tok = _rope_tables(positions).reshape(t, 3 * LANES)
    end_idx = jnp.minimum(jnp.arange(rows) * NSA_CMP_STRIDE + NSA_CMP_LEN - 1, s - 1)
    tab_end = _rope_tables(positions[:, end_idx])
    vec = lambda g: g.reshape(1, -1)

    def blocks16(a):
        return a.reshape(b, s, 2, HEAD_DIM).transpose(0, 2, 1, 3).reshape(b, 2, rows, NSA_CMP_STRIDE * HEAD_DIM)

    h = x
    for l in range(depth):
        qkv, aux = _proj(h.reshape(t, d), vec(norm_mix_pre[l]), _pad_gate_cols(w_in[l]).astype(BF16), tab_tok)
        aux3 = aux.reshape(b, s, 3 * LANES)
        pe_k = cmp_pe_k[l].reshape(2, -1)
        pe_v = cmp_pe_v[l].reshape(2, -1)
        kc = _compress(blocks16(aux3[..., 0:LANES]), pe_k, cmp_wk1[l].astype(BF16), cmp_wk2[l].astype(BF16),
                       tab_end, True)
        vc = _compress(blocks16(aux3[..., LANES:2 * LANES]), pe_v, cmp_wv1[l].astype(BF16),
                       cmp_wv2[l].astype(BF16), tab_end, False)
        o_moba = _moba(qkv, b, s)
        o_nsa = _nsa(qkv, aux, kc, vc, b, s)
        o_sb = _sb(qkv, b, s)
        h = _outproj(h.reshape(t, d), o_moba.reshape(t, -1), o_nsa.reshape(t, -1), o_sb.reshape(t, -1),
                     w_out[l].astype(BF16), vec(norm_mix_post[l]))
        wkv = jnp.concatenate([x_wk[l], x_wv[l]], axis=1).astype(BF16)
        kv = _memkv(mem.reshape(-1, d), vec(norm_mem[l]), wkv).reshape(b, -1, 2 * d)
        h = _cross(h.reshape(b, s, d), kv, x_wq[l].astype(BF16), x_wo[l].astype(BF16),
                   vec(norm_x_pre[l]), vec(norm_x_post[l]))
        h = _ffn(h, ffn_w_up[l].astype(BF16), ffn_conv_w[l], vec(ffn_conv_b[l]), ffn_w_down[l].astype(BF16),
                 vec(norm_ffn_pre[l]), vec(norm_ffn_post[l]))
    return h
```

```python
import functools
import math

import jax
import jax.numpy as jnp
import numpy as np
from jax import lax
from jax.experimental import pallas as pl
from jax.experimental.pallas import tpu as pltpu

F32 = jnp.float32
BF16 = jnp.bfloat16

LANES = 128
HEAD_DIM = 64
ROT_HALF = 8
ROPE_THETA = 500000.0
MOBA_BLOCK = 256
MOBA_TOPK = 3
NSA_CMP_LEN = 32
NSA_CMP_STRIDE = 16
NSA_SLC_BLOCK = 64
NSA_SLC_TOPN = 16
NSA_WINDOW = 512
NSA_FORCE_SCORE = 1e4
NSA_V_ROWS = 80
X_HEADS = 4
EPS = 1e-6
NEG = -1e30
BELOW_NEG = -3e38
SB_SKIP_TAIL = 110.0
LOG2E = 1.4426950408889634
VMEM_LIMIT = 56 * 1024 * 1024

G_MOBA_Q, G_MOBA_K, G_MOBA_V = 0, 2, 4
G_NSA_Q = 6
G_NSA_KC, G_NSA_VC, G_NSA_KS, G_NSA_VS, G_NSA_KW, G_NSA_VW = 10, 11, 12, 13, 14, 15
G_GATE = 16
G_SB_Q, G_SB_K, G_SB_V = 17, 19, 21
N_GROUPS = 23
ROPE_GROUPS = (0, 1, 2, 3, 6, 7, 8, 9, 12, 14)
GATE_COLS = 24


def _cparams(*sem):
    return pltpu.CompilerParams(dimension_semantics=sem, vmem_limit_bytes=VMEM_LIMIT)


def _dot(a, b):
    return jnp.dot(a, b, preferred_element_type=F32)


def _dot_nt(a, b):
    return lax.dot_general(a, b, (((1,), (1,)), ((), ())), preferred_element_type=F32)


def _dot_split(a_f32, b_bf16):
    hi = a_f32.astype(BF16)
    lo = (a_f32 - hi.astype(F32)).astype(BF16)
    return _dot(hi, b_bf16) + _dot(lo, b_bf16)


def _rms(x, g):
    ms = jnp.mean(x * x, axis=-1, keepdims=True)
    return x * lax.rsqrt(ms + EPS) * g


def _gelu_tanh(x):
    c = math.sqrt(2.0 / math.pi)
    return 0.5 * x * (1.0 + jnp.tanh(c * (x + 0.044715 * (x * x * x))))


def _rope(y, tab):
    c, sa, sb = tab[:, 0:LANES], tab[:, LANES:2 * LANES], tab[:, 2 * LANES:3 * LANES]
    return y * c + pltpu.roll(y, LANES - ROT_HALF, 1) * sa + pltpu.roll(y, ROT_HALF, 1) * sb


def _iota(shape, dim):
    return lax.broadcasted_iota(jnp.int32, shape, dim)


def _div_pow2(x, n):
    return lax.shift_right_logical(x, jnp.int32(int(math.log2(n))))


def _proj_kernel(x_ref, g_ref, w_ref, tab_ref, qkv_ref, aux_ref):
    xn = _rms(x_ref[...], g_ref[...]).astype(BF16)
    tab = tab_ref[...]
    ncol = w_ref.shape[1]
    for j in range(0, ncol, 2 * LANES):
        wdt = min(2 * LANES, ncol - j)
        acc = _dot(xn, w_ref[:, j:j + wdt])
        for s in range(wdt // LANES):
            grp = j // LANES + s
            y = acc[:, s * LANES:(s + 1) * LANES]
            if grp in ROPE_GROUPS:
                y = _rope(y, tab)
            if grp == G_NSA_KC:
                aux_ref[:, 0:LANES] = y
            elif grp == G_NSA_VC:
                aux_ref[:, LANES:2 * LANES] = y
            elif grp == G_GATE:
                aux_ref[:, 2 * LANES:3 * LANES] = 1.0 / (1.0 + jnp.exp(-y))
            qkv_ref[:, grp * LANES:(grp + 1) * LANES] = y.astype(BF16)


def _proj(h2, g, w, tab, tm=512):
    t, d = h2.shape
    ncol = w.shape[1]
    return pl.pallas_call(
        _proj_kernel,
        grid=(t // tm,),
        in_specs=[
            pl.BlockSpec((tm, d), lambda i: (i, 0)),
            pl.BlockSpec((1, d), lambda i: (0, 0)),
            pl.BlockSpec((d, ncol), lambda i: (0, 0)),
            pl.BlockSpec((tm, 3 * LANES), lambda i: (i, 0)),
        ],
        out_specs=[
            pl.BlockSpec((tm, ncol), lambda i: (i, 0)),
            pl.BlockSpec((tm, 3 * LANES), lambda i: (i, 0)),
        ],
        out_shape=[
            jax.ShapeDtypeStruct((t, ncol), BF16),
            jax.ShapeDtypeStruct((t, 3 * LANES), F32),
        ],
        compiler_params=_cparams("parallel"),
        name="proj",
    )(h2, g, w, tab)


def _topk_rows(score, k):
    rowf = _iota(score.shape, 0).astype(F32)
    for _ in range(k):
        m = jnp.max(score, axis=0, keepdims=True)
        idx = jnp.min(jnp.where(score == m, rowf, 1e9), axis=0, keepdims=True)
        score = jnp.where(rowf == idx, BELOW_NEG, score)
    return score == BELOW_NEG


def _moba_kernel(q_ref, k_ref, e_ref, vt_ref, o_ref, km_ref, acc_ref, sa_ref, sb_ref, *, nb, topk, per_tile):
    c = pl.program_id(2)
    tq = q_ref.shape[1]
    L = MOBA_BLOCK
    tk = per_tile * L

    @pl.when(c == 0)
    def _():
        km_ref[...] = jnp.zeros_like(km_ref)
        for j in range(nb):
            km = jnp.mean(k_ref[0, j * L:(j + 1) * L, :].astype(F32), axis=0, keepdims=True)
            hi = km.astype(BF16).astype(F32)
            km_ref[0, j:j + 1, :] = hi
            km_ref[1, j:j + 1, :] = km - hi

    drow = _iota((LANES, 1), 0)
    q2t = q_ref[0].astype(F32).T
    blk = _iota((LANES, tq), 0)
    kmh = km_ref[0].astype(BF16)
    kml = km_ref[1].astype(BF16)
    qts, biases = [], []
    for h in range(2):
        qt = jnp.where(_div_pow2(drow, HEAD_DIM) == h, q2t, 0.0)
        qb = qt.astype(BF16)
        gate = _dot(kmh, qb) + _dot(kml, qb)
        sel = _topk_rows(jnp.where(blk < c, gate, NEG), topk)
        biases.append(jnp.where(sel | (blk == c), 0.0, NEG).astype(BF16))
        qts.append((qt * (HEAD_DIM ** -0.5 * LOG2E)).astype(BF16))
    qa = jnp.concatenate([jnp.concatenate(qts, axis=1), jnp.concatenate(biases, axis=1)], axis=0)

    def scores(t):
        k0 = pl.multiple_of(t * tk, tk)
        return _dot(jnp.concatenate([k_ref[0, pl.ds(k0, tk), :], e_ref[pl.ds(k0, tk), :]], axis=1), qa)

    def tile(t, m, l, cur_ref, nxt_ref, last):
        st = cur_ref[...]
        if last:
            kpos = t * tk + _iota((tk, tq), 0)
            qpos = c * L + _iota((tk, tq), 1)
            tri = jnp.where(kpos <= qpos, 0.0, NEG)
            st = st + jnp.concatenate([tri, tri], axis=1)
        else:
            nxt_ref[...] = scores(t + 1)
        m_new = jnp.maximum(m, jnp.max(st, axis=0, keepdims=True))
        alpha = jnp.exp2(m - m_new)
        p = jnp.exp2(st - m_new)
        acc_ref[...] = alpha * acc_ref[...] + _dot(vt_ref[0, 0, t], p.astype(BF16))
        return m_new, alpha * l + jnp.sum(p, axis=0, keepdims=True)

    acc_ref[...] = jnp.zeros_like(acc_ref)
    sa_ref[...] = scores(0)
    n_full = c // per_tile

    def pair(u, ml):
        ml = tile(2 * u, ml[0], ml[1], sa_ref, sb_ref, False)
        return tile(2 * u + 1, ml[0], ml[1], sb_ref, sa_ref, False)

    init = (jnp.full((1, 2 * tq), NEG, F32), jnp.zeros((1, 2 * tq), F32))
    ml = lax.fori_loop(0, n_full // 2, pair, init)

    def odd_tail(ml):
        ml = tile(n_full - 1, ml[0], ml[1], sa_ref, sb_ref, False)
        return tile(n_full, ml[0], ml[1], sb_ref, sa_ref, True)

    m, l = lax.cond(n_full % 2 == 1, odd_tail, lambda ml: tile(n_full, ml[0], ml[1], sa_ref, sb_ref, True), ml)
    o = acc_ref[...] / l
    o_ref[0] = jnp.where(drow < HEAD_DIM, o[:, 0:tq], o[:, tq:2 * tq]).T.astype(o_ref.dtype)


def _key_major(a3, blk):
    b, s, c = a3.shape
    return a3.reshape(b, s // blk, blk, c // LANES, LANES).transpose(0, 3, 1, 4, 2)


def _block_onehot(s, blk):
    return jnp.asarray(np.arange(s)[:, None] // blk == np.arange(LANES)[None, :], BF16)


def _moba(qkv, b, s, per_tile=4):
    nb = s // MOBA_BLOCK
    assert nb % per_tile == 0 and nb <= LANES
    topk = max(1, min(MOBA_TOPK, nb - 1))
    tq = MOBA_BLOCK
    tk = per_tile * MOBA_BLOCK
    qkv3 = qkv.reshape(b, s, -1)
    gq, gk, gv = G_MOBA_Q, G_MOBA_K, G_MOBA_V
    vt = _key_major(qkv3[..., gv * LANES:(gv + 2) * LANES], tk)
    return pl.pallas_call(
        functools.partial(_moba_kernel, nb=nb, topk=topk, per_tile=per_tile),
        grid=(b, 2, s // tq),
        in_specs=[
            pl.BlockSpec((1, tq, LANES), lambda i, p, c: (i, c, gq + p)),
            pl.BlockSpec((1, s, LANES), lambda i, p, c: (i, 0, gk + p)),
            pl.BlockSpec((s, LANES), lambda i, p, c: (0, 0)),
            pl.BlockSpec((1, 1, s // tk, LANES, tk), lambda i, p, c: (i, p, 0, 0, 0)),
        ],
        out_specs=pl.BlockSpec((1, tq, LANES), lambda i, p, c: (i, c, p)),
        out_shape=jax.ShapeDtypeStruct((b, s, 2 * LANES), BF16),
        scratch_shapes=[pltpu.VMEM((2, LANES, LANES), F32), pltpu.VMEM((LANES, 2 * tq), F32),
                        pltpu.VMEM((tk, 2 * tq), F32), pltpu.VMEM((tk, 2 * tq), F32)],
        compiler_params=_cparams("parallel", "parallel", "arbitrary"),
        name="moba",
    )(qkv3, qkv3, _block_onehot(s, MOBA_BLOCK), vt)


def _cmp_kernel(x_ref, pe_ref, w1_ref, w2_ref, tab_ref, o_ref, *, rope):
    rows = x_ref.shape[2]
    half = w1_ref.shape[0] // 2
    outs = []
    for g in range(2):
        x = x_ref[0, g]
        xa = (x + pe_ref[0:1, :]).astype(BF16)
        xb = (x + pe_ref[1:2, :]).astype(BF16)
        a = _dot(xa, w1_ref[0:half, :])
        bm = _dot(xb, w1_ref[half:2 * half, :])
        h1 = a + pltpu.roll(bm, rows - 1, 0)
        outs.append(_dot(_gelu_tanh(h1).astype(BF16), w2_ref[...]))
    y = jnp.concatenate(outs, axis=1)
    if rope:
        y = _rope(y, tab_ref[0])
    o_ref[0] = y.astype(o_ref.dtype)


def _compress(x4, pe2, w1, w2, tab, rope):
    b, g, rows, width = x4.shape
    hid = w1.shape[1]
    return pl.pallas_call(
        functools.partial(_cmp_kernel, rope=rope),
        grid=(b,),
        in_specs=[
            pl.BlockSpec((1, g, rows, width), lambda i: (i, 0, 0, 0)),
            pl.BlockSpec((2, width), lambda i: (0, 0)),
            pl.BlockSpec((2 * width, hid), lambda i: (0, 0)),
            pl.BlockSpec((hid, HEAD_DIM), lambda i: (0, 0)),
            pl.BlockSpec((1, rows, 3 * LANES), lambda i: (i, 0, 0)),
        ],
        out_specs=pl.BlockSpec((1, rows, LANES), lambda i: (i, 0, 0)),
        out_shape=jax.ShapeDtypeStruct((b, rows, LANES), BF16),
        compiler_params=_cparams("parallel"),
        name="nsa_compress",
    )(x4, pe2, w1, w2, tab)


def _nsa_kernel(q_ref, kc_ref, vct_ref, ks_ref, e_ref, vst_ref, kw_ref, vwt_ref, gt_ref, ovt_ref, o_ref,
                gts_ref, acc_ref, ow_ref, sa_ref, sb_ref, *, topn, tk):
    g = pl.program_id(1)
    c = pl.program_id(2)
    tq = q_ref.shape[1]
    nrep = q_ref.shape[2] // HEAD_DIM
    ncp = kc_ref.shape[1]
    q0 = c * tq
    nwin = (NSA_WINDOW + tq) // LANES

    drow = _iota((LANES, 1), 0)
    in_g = _div_pow2(drow, HEAD_DIM) == g
    q2 = q_ref[0].astype(F32)
    cols = []
    for r in range(nrep):
        qpt = q2[:, (r // 2) * LANES:(r // 2 + 1) * LANES].T
        qpt = jnp.where(g == r % 2, qpt, pltpu.roll(qpt, HEAD_DIM, 0))
        cols.append(jnp.where(in_g, qpt, 0.0) * (HEAD_DIM ** -0.5 * LOG2E))
    qst = jnp.concatenate(cols, axis=1).astype(BF16)
    qpos = q0 + _iota((1, tq), 1)
    qpos_w = jnp.concatenate([qpos] * nrep, axis=1)

    s_c = _dot(kc_ref[0], qst)
    cmp_end = _iota((ncp, 1), 0) * NSA_CMP_STRIDE + (NSA_CMP_LEN - 1)
    mask_c = cmp_end <= qpos_w
    sm = jnp.where(mask_c, s_c, NEG)
    m = jnp.max(sm, axis=0, keepdims=True)
    e = jnp.where(mask_c, jnp.exp2(sm - m), 0.0)
    l = jnp.sum(e, axis=0, keepdims=True)
    p_c = e / jnp.where(l > 0.0, l, 1.0)

    psum = p_c[:, 0:tq]
    for r in range(1, nrep):
        psum = psum + p_c[:, r * tq:(r + 1) * tq]
    hi = psum.astype(BF16)
    lo = (psum - hi.astype(F32)).astype(BF16)
    imp = _dot(ovt_ref[...], hi) + _dot(ovt_ref[...], lo)
    o_c = _dot(vct_ref[0], p_c.astype(BF16))
    jb = _iota((LANES, tq), 0)
    cur = _div_pow2(qpos, NSA_SLC_BLOCK)
    forced = (jb == 0) | (jb == cur) | (jb == cur - 1)
    imp = jnp.where(forced, NSA_FORCE_SCORE, imp)
    imp = jnp.where(jb <= cur, imp, NEG)
    bias = jnp.where(_topk_rows(imp, topn), 0.0, NEG).astype(BF16)

    st0 = pl.multiple_of(jnp.maximum(q0 - NSA_WINDOW, 0), LANES)
    dlt = qpos_w - (st0 + _iota((nwin * LANES, 1), 0))
    s_w = jnp.where((dlt >= 0) & (dlt < NSA_WINDOW), _dot(kw_ref[0, pl.ds(st0, nwin * LANES), :], qst), NEG)
    p_w = jnp.exp2(s_w - jnp.max(s_w, axis=0, keepdims=True))
    blk0 = st0 // LANES
    o_w = _dot(vwt_ref[0, blk0], p_w[0:LANES].astype(BF16))
    for i in range(1, nwin):
        o_w = o_w + _dot(vwt_ref[0, blk0 + i], p_w[i * LANES:(i + 1) * LANES].astype(BF16))
    ow_ref[...] = o_w / jnp.sum(p_w, axis=0, keepdims=True)

    qa = jnp.concatenate([qst, jnp.concatenate([bias] * nrep, axis=1)], axis=0)

    def scores(t):
        k0 = pl.multiple_of(t * tk, tk)
        return _dot(jnp.concatenate([ks_ref[0, pl.ds(k0, tk), :], e_ref[pl.ds(k0, tk), :]], axis=1), qa)

    def tile(t, m, cur_ref, nxt_ref, diagonal):
        st = cur_ref[...]
        if diagonal:
            st = jnp.where(t * tk + _iota((tk, 1), 0) <= qpos_w, st, NEG)
        else:
            nxt_ref[...] = scores(t + 1)
        m_new = jnp.maximum(m, jnp.max(st, axis=0, keepdims=True))
        p = jnp.exp2(st - m_new)
        acc_ref[...] = jnp.exp2(m - m_new) * acc_ref[...] + _dot(vst_ref[0, 0, t], p.astype(BF16))
        return m_new

    acc_ref[...] = jnp.zeros_like(acc_ref)
    sa_ref[...] = scores(0)
    n_full = q0 // tk

    def pair(u, m):
        return tile(2 * u + 1, tile(2 * u, m, sa_ref, sb_ref, False), sb_ref, sa_ref, False)

    m_s = lax.fori_loop(0, n_full // 2, pair, jnp.full((1, nrep * tq), NEG, F32))

    def odd_tail(m):
        return tile(n_full, tile(n_full - 1, m, sa_ref, sb_ref, False), sb_ref, sa_ref, True)

    lax.cond(n_full % 2 == 1, odd_tail, lambda m: tile(n_full, m, sa_ref, sb_ref, True), m_s)
    acc = acc_ref[...]
    o_own = acc[0:HEAD_DIM] / acc[HEAD_DIM:HEAD_DIM + 1]
    gap = jnp.zeros_like(o_own)
    o_s = jnp.where(g == 0, jnp.concatenate([o_own, gap], axis=0), jnp.concatenate([gap, o_own], axis=0))
    o_w = ow_ref[...]

    gts_ref[...] = gt_ref[0].T
    outs = []
    for r in range(nrep):
        base = (g * nrep + r) * 3
        cs = slice(r * tq, (r + 1) * tq)
        o_r = (gts_ref[pl.ds(base, 1), :] * o_c[:, cs] + gts_ref[pl.ds(base + 1, 1), :] * o_s[:, cs]
               + gts_ref[pl.ds(base + 2, 1), :] * o_w[:, cs])
        outs.append(jnp.where(g == r % 2, o_r, pltpu.roll(o_r, HEAD_DIM, 0)))
    pairs = [jnp.where(drow < HEAD_DIM, outs[2 * i], outs[2 * i + 1]).T for i in range(nrep // 2)]
    o_ref[0] = jnp.concatenate(pairs, axis=1).astype(o_ref.dtype)


def _nsa_overlap_t(s, ncp):
    nc = (s - NSA_CMP_LEN) // NSA_CMP_STRIDE + 1
    nsb = s // NSA_SLC_BLOCK
    cs = np.arange(nc) * NSA_CMP_STRIDE
    bs = np.arange(nsb) * NSA_SLC_BLOCK
    ov = (cs[None, :] < bs[:, None] + NSA_SLC_BLOCK) & (cs[None, :] + NSA_CMP_LEN > bs[:, None])
    out = np.zeros((LANES, ncp), np.float32)
    out[:nsb, :nc] = ov
    return jnp.asarray(out, BF16)


def _nsa(qkv, aux, kc, vc, b, s, tq=256, tk=512):
    qkv3 = qkv.reshape(b, s, -1)
    aux3 = aux.reshape(b, s, -1)
    ncp = kc.shape[1]
    nsb = s // NSA_SLC_BLOCK
    assert nsb <= LANES and s >= NSA_WINDOW + tq and s % tk == 0 and tk % tq == 0
    topn = min(NSA_SLC_TOPN, nsb)
    vct = vc.transpose(0, 2, 1)
    vst = _key_major(qkv3[..., G_NSA_VS * LANES:(G_NSA_VS + 1) * LANES], tk)[:, 0]
    vst = vst.reshape(b, s // tk, 2, HEAD_DIM, tk).transpose(0, 2, 1, 3, 4)
    pad = jnp.zeros((b, 2, s // tk, NSA_V_ROWS - HEAD_DIM, tk), BF16).at[:, :, :, 0].set(1.0)
    vst = jnp.concatenate([vst, pad], axis=3)
    vwt = _key_major(qkv3[..., G_NSA_VW * LANES:(G_NSA_VW + 1) * LANES], LANES)[:, 0]
    full = lambda grp: pl.BlockSpec((1, s, LANES), lambda i, g, c: (i, 0, grp))
    return pl.pallas_call(
        functools.partial(_nsa_kernel, topn=topn, tk=tk),
        grid=(b, 2, s // tq),
        in_specs=[
            pl.BlockSpec((1, tq, 2 * LANES), lambda i, g, c: (i, c, G_NSA_Q // 2 + g)),
            pl.BlockSpec((1, ncp, LANES), lambda i, g, c: (i, 0, 0)),
            pl.BlockSpec((1, LANES, ncp), lambda i, g, c: (i, 0, 0)),
            full(G_NSA_KS),
            pl.BlockSpec((s, LANES), lambda i, g, c: (0, 0)),
            pl.BlockSpec((1, 1, s // tk, NSA_V_ROWS, tk), lambda i, g, c: (i, g, 0, 0, 0)),
            full(G_NSA_KW),
            pl.BlockSpec((1, s // LANES, LANES, LANES), lambda i, g, c: (i, 0, 0, 0)),
            pl.BlockSpec((1, tq, LANES), lambda i, g, c: (i, c, 2)),
            pl.BlockSpec((LANES, ncp), lambda i, g, c: (0, 0)),
        ],
        out_specs=pl.BlockSpec((1, tq, 2 * LANES), lambda i, g, c: (i, c, g)),
        out_shape=jax.ShapeDtypeStruct((b, s, 4 * LANES), BF16),
        scratch_shapes=[pltpu.VMEM((LANES, tq), F32), pltpu.VMEM((NSA_V_ROWS, 4 * tq), F32),
                        pltpu.VMEM((LANES, 4 * tq), F32), pltpu.VMEM((tk, 4 * tq), F32),
                        pltpu.VMEM((tk, 4 * tq), F32)],
        compiler_params=_cparams("parallel", "parallel", "arbitrary"),
        name="nsa",
    )(qkv3, kc, vct, qkv3, _block_onehot(s, NSA_SLC_BLOCK), vst, qkv3, vwt, aux3, _nsa_overlap_t(s, ncp))


def _sb_kernel(q_ref, k_ref, v_ref, u_ref, o_ref, *, tk):
    c = pl.program_id(2)
    tq = q_ref.shape[1]
    lane = _iota((1, LANES), 1)
    q2 = q_ref[0]
    qhs = [jnp.where(_div_pow2(lane, HEAD_DIM) == h, q2, jnp.zeros_like(q2)) * jnp.asarray(HEAD_DIM ** -0.5, BF16)
           for h in range(2)]
    qpos = c * tq + _iota((tq, tk), 0)
    kcol = _iota((tq, tk), 1)
    suffix = u_ref[...]

    def cond(st):
        return (st[0] > 0) & (st[1] < SB_SKIP_TAIL)

    def body(st):
        hi = st[0]
        lo = pl.multiple_of(jnp.maximum(hi - tk, 0), tq)
        kj = k_ref[0, pl.ds(lo, tk), :]
        vj = v_ref[0, pl.ds(lo, tk), :]
        kpos = lo + kcol
        valid = (kpos < qpos) & (kpos < hi)
        zs = [_dot_nt(qhs[h], kj) for h in range(2)]
        sps = [jnp.where(valid, jnp.maximum(z, 0.0) + jnp.log(1.0 + jnp.exp(-jnp.abs(z))), 0.0) for z in zs]
        incls = [_dot_split(sp, suffix) for sp in sps]
        ws = [jnp.where(valid, jnp.exp(zs[h] - incls[h] - st[2 + 2 * h]), 0.0) for h in range(2)]
        out = []
        for h in range(2):
            out += [st[2 + 2 * h] + incls[h][:, 0:1], st[3 + 2 * h] + _dot(ws[h].astype(BF16), vj)]
        return (lo, jnp.minimum(jnp.min(out[0]), jnp.min(out[2])), *out)

    zero_t = jnp.zeros((tq, 1), F32)
    zero_a = jnp.zeros((tq, LANES), F32)
    res = lax.while_loop(cond, body, ((c + 1) * tq, jnp.float32(0.0), zero_t, zero_a, zero_t, zero_a))
    o_ref[0] = jnp.where(lane < HEAD_DIM, res[3], res[5]).astype(o_ref.dtype)


def _sb(qkv, b, s, tq=256, tk=512):
    assert s >= tk and tk % tq == 0
    qkv3 = qkv.reshape(b, s, -1)
    gq, gk, gv = G_SB_Q, G_SB_K, G_SB_V
    suffix = jnp.asarray(np.arange(tk)[:, None] >= np.arange(tk)[None, :], BF16)
    return pl.pallas_call(
        functools.partial(_sb_kernel, tk=tk),
        grid=(b, 2, s // tq),
        in_specs=[
            pl.BlockSpec((1, tq, LANES), lambda i, p, c: (i, c, gq + p)),
            pl.BlockSpec((1, s, LANES), lambda i, p, c: (i, 0, gk + p)),
            pl.BlockSpec((1, s, LANES), lambda i, p, c: (i, 0, gv + p)),
            pl.BlockSpec((tk, tk), lambda i, p, c: (0, 0)),
        ],
        out_specs=pl.BlockSpec((1, tq, LANES), lambda i, p, c: (i, c, p)),
        out_shape=jax.ShapeDtypeStruct((b, s, 2 * LANES), BF16),
        compiler_params=_cparams("parallel", "parallel", "arbitrary"),
        name="stick_breaking",
    )(qkv3, qkv3, qkv3, suffix)


def _outproj_kernel(h_ref, om_ref, on_ref, os_ref, w_ref, g_ref, o_ref):
    dm, dn = om_ref.shape[1], on_ref.shape[1]
    mix = (_dot(om_ref[...], w_ref[0:dm, :]) + _dot(on_ref[...], w_ref[dm:dm + dn, :])
           + _dot(os_ref[...], w_ref[dm + dn:, :]))
    o_ref[...] = h_ref[...] + _rms(mix, g_ref[...])


def _outproj(h2, om, on, osb, w, g, tm=1024):
    t, d = h2.shape
    row = lambda n: pl.BlockSpec((tm, n), lambda i: (i, 0))
    return pl.pallas_call(
        _outproj_kernel,
        grid=(t // tm,),
        in_specs=[row(d), row(om.shape[1]), row(on.shape[1]), row(osb.shape[1]),
                  pl.BlockSpec(w.shape, lambda i: (0, 0)), pl.BlockSpec((1, d), lambda i: (0, 0))],
        out_specs=row(d),
        out_shape=jax.ShapeDtypeStruct((t, d), F32),
        compiler_params=_cparams("parallel"),
        name="mixer_out",
    )(h2, om, on, osb, w, g)


def _memkv_kernel(m_ref, g_ref, w_ref, o_ref):
    o_ref[...] = _dot(_rms(m_ref[...], g_ref[...]).astype(BF16), w_ref[...]).astype(o_ref.dtype)


def _memkv(mem2, g, wkv, tm=256):
    t, d = mem2.shape
    n = wkv.shape[1]
    return pl.pallas_call(
        _memkv_kernel,
        grid=(t // tm,),
        in_specs=[pl.BlockSpec((tm, d), lambda i: (i, 0)), pl.BlockSpec((1, d), lambda i: (0, 0)),
                  pl.BlockSpec((d, n), lambda i: (0, 0))],
        out_specs=pl.BlockSpec((tm, n), lambda i: (i, 0)),
        out_shape=jax.ShapeDtypeStruct((t, n), BF16),
        compiler_params=_cparams("parallel"),
        name="mem_kv",
    )(mem2, g, wkv)


def _cross_kernel(h_ref, k_ref, v_ref, wq_ref, wo_ref, gpre_ref, gpost_ref, o_ref):
    h = h_ref[0]
    d = h.shape[1]
    hd = d // X_HEADS
    q = _dot(_rms(h, gpre_ref[...]).astype(BF16), wq_ref[...])
    cols = [slice(x * hd, (x + 1) * hd) for x in range(X_HEADS)]
    ss = [_dot_nt((q[:, c] * hd ** -0.5).astype(BF16), k_ref[0, :, c]) for c in cols]
    ps = []
    for s in ss:
        p = jnp.exp(s - jnp.max(s, axis=1, keepdims=True))
        ps.append((p / jnp.sum(p, axis=1, keepdims=True)).astype(BF16))
    outs = [_dot(p, v_ref[0, :, c]).astype(BF16) for p, c in zip(ps, cols)]
    o = _dot(jnp.concatenate(outs, axis=1), wo_ref[...])
    o_ref[0] = h + _rms(o, gpost_ref[...])


def _cross(h3, kv3, wq, wo, gpre, gpost, tm=512):
    b, s, d = h3.shape
    m = kv3.shape[1]
    vec = pl.BlockSpec((1, d), lambda i, j: (0, 0))
    mat = pl.BlockSpec((d, d), lambda i, j: (0, 0))
    return pl.pallas_call(
        _cross_kernel,
        grid=(b, s // tm),
        in_specs=[pl.BlockSpec((1, tm, d), lambda i, j: (i, j, 0)),
                  pl.BlockSpec((1, m, d), lambda i, j: (i, 0, 0)),
                  pl.BlockSpec((1, m, d), lambda i, j: (i, 0, 1)),
                  mat, mat, vec, vec],
        out_specs=pl.BlockSpec((1, tm, d), lambda i, j: (i, j, 0)),
        out_shape=jax.ShapeDtypeStruct((b, s, d), F32),
        compiler_params=_cparams("parallel", "parallel"),
        name="cross_attn",
    )(h3, kv3, kv3, wq, wo, gpre, gpost)


def _ffn_kernel(h_ref, wup_ref, cw_ref, cb_ref, wdn_ref, gpre_ref, gpost_ref, o_ref, tail_ref, acc_ref,
                *, tc):
    j = pl.program_id(1)
    h = h_ref[0]
    tm = h.shape[0]
    dff = wdn_ref.shape[0]

    @pl.when(j == 0)
    def _():
        tail_ref[...] = jnp.zeros_like(tail_ref)

    xn = _rms(h, gpre_ref[...]).astype(BF16)
    row = _iota((tm, tc), 0)

    def up(c0):
        return _dot(xn, wup_ref[:, c0:c0 + tc]), _dot(xn, wup_ref[:, dff + c0:dff + c0 + tc])

    nxt = up(0)
    for c0 in range(0, dff, tc):
        cols = slice(c0, c0 + tc)
        a, gate = nxt
        if c0 + tc < dff:
            nxt = up(c0 + tc)
        prev1 = tail_ref[7:8, cols]
        prev2 = tail_ref[6:7, cols]
        a1 = jnp.where(row == 0, prev1, pltpu.roll(a, 1, 0))
        a2 = jnp.where(row == 0, prev2, jnp.where(row == 1, prev1, pltpu.roll(a, 2, 0)))
        tail_ref[:, cols] = a[tm - 8:tm, :]
        y = cw_ref[0:1, cols] * a2 + cw_ref[1:2, cols] * a1 + cw_ref[2:3, cols] * a + cb_ref[:, cols]
        part = _dot((_gelu_tanh(y) * gate).astype(BF16), wdn_ref[cols, :])
        if c0 == 0:
            acc_ref[...] = part
        else:
            acc_ref[...] += part
    o_ref[0] = h + _rms(acc_ref[...], gpost_ref[...])


def _ffn(h3, wup, cw, cb, wdn, gpre, gpost, tm=256, tc=256):
    b, s, d = h3.shape
    dff = wdn.shape[0]
    const = lambda shape: pl.BlockSpec(shape, lambda i, j: (0, 0))
    return pl.pallas_call(
        functools.partial(_ffn_kernel, tc=tc),
        grid=(b, s // tm),
        in_specs=[pl.BlockSpec((1, tm, d), lambda i, j: (i, j, 0)),
                  const(wup.shape), const(cw.shape), const(cb.shape), const(wdn.shape),
                  const((1, d)), const((1, d))],
        out_specs=pl.BlockSpec((1, tm, d), lambda i, j: (i, j, 0)),
        out_shape=jax.ShapeDtypeStruct((b, s, d), F32),
        scratch_shapes=[pltpu.VMEM((8, dff), F32), pltpu.VMEM((tm, d), F32)],
        compiler_params=_cparams("parallel", "arbitrary"),
        name="conv_ffn",
    )(h3, wup, cw, cb, wdn, gpre, gpost)


def _rope_tables(pos):
    inv_freq = ROPE_THETA ** (-jnp.arange(ROT_HALF, dtype=F32) / ROT_HALF)
    ang = pos.astype(F32)[..., None] * inv_freq
    cs = jnp.concatenate([jnp.cos(ang), jnp.sin(ang)], axis=-1)
    dim = np.arange(LANES) % HEAD_DIM
    spread = np.zeros((2 * ROT_HALF, 3 * LANES), np.float32)
    ones = np.zeros((3 * LANES,), np.float32)
    for lane in range(LANES):
        d = dim[lane]
        if d < ROT_HALF:
            spread[d, lane] = 1.0
            spread[ROT_HALF + d, LANES + lane] = -1.0
        elif d < 2 * ROT_HALF:
            spread[d - ROT_HALF, lane] = 1.0
            spread[d, 2 * LANES + lane] = 1.0
        else:
            ones[lane] = 1.0
    return jnp.dot(cs, jnp.asarray(spread), precision=lax.Precision.HIGHEST) + jnp.asarray(ones)


def _pad_gate_cols(w_in):
    cut = G_GATE * LANES
    pad = jnp.zeros((w_in.shape[0], LANES - GATE_COLS), w_in.dtype)
    return jnp.concatenate([w_in[:, :cut + GATE_COLS], pad, w_in[:, cut + GATE_COLS:]], axis=1)


def kernel(x, mem, positions, norm_mix_pre, norm_mix_post, w_in, w_out, cmp_pe_k, cmp_pe_v, cmp_wk1, cmp_wk2, cmp_wv1, cmp_wv2, norm_x_pre, norm_x_post, norm_mem, x_wq, x_wk, x_wv, x_wo, norm_ffn_pre, norm_ffn_post, ffn_w_up, ffn_conv_w, ffn_conv_b, ffn_w_down):
    b, s, d = x.shape
    depth = w_in.shape[0]
    t = b * s
    rows = s // NSA_CMP_STRIDE
    tab_tok = _rope_tables(positions).reshape(t, 3 * LANES)
    end_idx = jnp.minimum(jnp.arange(rows) * NSA_CMP_STRIDE + NSA_CMP_LEN - 1, s - 1)
    tab_end = _rope_tables(positions[:, end_idx])
    vec = lambda g: g.reshape(1, -1)

    def blocks16(a):
        return a.reshape(b, s, 2, HEAD_DIM).transpose(0, 2, 1, 3).reshape(b, 2, rows, NSA_CMP_STRIDE * HEAD_DIM)

    h = x
    for l in range(depth):
        qkv, aux = _proj(h.reshape(t, d), vec(norm_mix_pre[l]), _pad_gate_cols(w_in[l]).astype(BF16), tab_tok)
        aux3 = aux.reshape(b, s, 3 * LANES)
        pe_k = cmp_pe_k[l].reshape(2, -1)
        pe_v = cmp_pe_v[l].reshape(2, -1)
        kc = _compress(blocks16(aux3[..., 0:LANES]), pe_k, cmp_wk1[l].astype(BF16), cmp_wk2[l].astype(BF16),
                       tab_end, True)
        vc = _compress(blocks16(aux3[..., LANES:2 * LANES]), pe_v, cmp_wv1[l].astype(BF16),
                       cmp_wv2[l].astype(BF16), tab_end, False)
        o_moba = _moba(qkv, b, s)
        o_nsa = _nsa(qkv, aux, kc, vc, b, s)
        o_sb = _sb(qkv, b, s)
        h = _outproj(h.reshape(t, d), o_moba.reshape(t, -1), o_nsa.reshape(t, -1), o_sb.reshape(t, -1),
                     w_out[l].astype(BF16), vec(norm_mix_post[l]))
        wkv = jnp.concatenate([x_wk[l], x_wv[l]], axis=1).astype(BF16)
        kv = _memkv(mem.reshape(-1, d), vec(norm_mem[l]), wkv).reshape(b, -1, 2 * d)
        h = _cross(h.reshape(b, s, d), kv, x_wq[l].astype(BF16), x_wo[l].astype(BF16),
                   vec(norm_x_pre[l]), vec(norm_x_post[l]))
        h = _ffn(h, ffn_w_up[l].astype(BF16), ffn_conv_w[l], vec(ffn_conv_b[l]), ffn_w_down[l].astype(BF16),
                 vec(norm_ffn_pre[l]), vec(norm_ffn_post[l]))
    return h
```

```python
import functools
import math

import jax
import jax.numpy as jnp
import numpy as np
from jax import lax
from jax.experimental import pallas as pl
from jax.experimental.pallas import tpu as pltpu

F32 = jnp.float32
BF16 = jnp.bfloat16

LANES = 128
HEAD_DIM = 64
ROT_HALF = 8
ROPE_THETA = 500000.0
MOBA_BLOCK = 256
MOBA_TOPK = 3
NSA_CMP_LEN = 32
NSA_CMP_STRIDE = 16
NSA_SLC_BLOCK = 64
NSA_SLC_TOPN = 16
NSA_WINDOW = 512
NSA_FORCE_SCORE = 1e4
NSA_V_ROWS = 80
X_HEADS = 4
EPS = 1e-6
NEG = -1e30
BELOW_NEG = -3e38
SB_SKIP_TAIL = 110.0
LOG2E = 1.4426950408889634
VMEM_LIMIT = 56 * 1024 * 1024

G_MOBA_Q, G_MOBA_K, G_MOBA_V = 0, 2, 4
G_NSA_Q = 6
G_NSA_KC, G_NSA_VC, G_NSA_KS, G_NSA_VS, G_NSA_KW, G_NSA_VW = 10, 11, 12, 13, 14, 15
G_GATE = 16
G_SB_Q, G_SB_K, G_SB_V = 17, 19, 21
N_GROUPS = 23
ROPE_GROUPS = (0, 1, 2, 3, 6, 7, 8, 9, 12, 14)
GATE_COLS = 24


def _cparams(*sem):
    return pltpu.CompilerParams(dimension_semantics=sem, vmem_limit_bytes=VMEM_LIMIT)


def _dot(a, b):
    return jnp.dot(a, b, preferred_element_type=F32)


def _dot_nt(a, b):
    return lax.dot_general(a, b, (((1,), (1,)), ((), ())), preferred_element_type=F32)


def _dot_split(a_f32, b_bf16):
    hi = a_f32.astype(BF16)
    lo = (a_f32 - hi.astype(F32)).astype(BF16)
    return _dot(hi, b_bf16) + _dot(lo, b_bf16)


def _rms(x, g):
    ms = jnp.mean(x * x, axis=-1, keepdims=True)
    return x * lax.rsqrt(ms + EPS) * g


def _gelu_tanh(x):
    c = math.sqrt(2.0 / math.pi)
    return 0.5 * x * (1.0 + jnp.tanh(c * (x + 0.044715 * (x * x * x))))


def _rope(y, tab):
    c, sa, sb = tab[:, 0:LANES], tab[:, LANES:2 * LANES], tab[:, 2 * LANES:3 * LANES]
    return y * c + pltpu.roll(y, LANES - ROT_HALF, 1) * sa + pltpu.roll(y, ROT_HALF, 1) * sb


def _iota(shape, dim):
    return lax.broadcasted_iota(jnp.int32, shape, dim)


def _div_pow2(x, n):
    return lax.shift_right_logical(x, jnp.int32(int(math.log2(n))))


def _proj_kernel(x_ref, g_ref, w_ref, tab_ref, qkv_ref, aux_ref):
    xn = _rms(x_ref[...], g_ref[...]).astype(BF16)
    tab = tab_ref[...]
    ncol = w_ref.shape[1]
    for j in range(0, ncol, 2 * LANES):
        wdt = min(2 * LANES, ncol - j)
        acc = _dot(xn, w_ref[:, j:j + wdt])
        for s in range(wdt // LANES):
            grp = j // LANES + s
            y = acc[:, s * LANES:(s + 1) * LANES]
            if grp in ROPE_GROUPS:
                y = _rope(y, tab)
            if grp == G_NSA_KC:
                aux_ref[:, 0:LANES] = y
            elif grp == G_NSA_VC:
                aux_ref[:, LANES:2 * LANES] = y
            elif grp == G_GATE:
                aux_ref[:, 2 * LANES:3 * LANES] = 1.0 / (1.0 + jnp.exp(-y))
            qkv_ref[:, grp * LANES:(grp + 1) * LANES] = y.astype(BF16)


def _proj(h2, g, w, tab, tm=512):
    t, d = h2.shape
    ncol = w.shape[1]
    return pl.pallas_call(
        _proj_kernel,
        grid=(t // tm,),
        in_specs=[
            pl.BlockSpec((tm, d), lambda i: (i, 0)),
            pl.BlockSpec((1, d), lambda i: (0, 0)),
            pl.BlockSpec((d, ncol), lambda i: (0, 0)),
            pl.BlockSpec((tm, 3 * LANES), lambda i: (i, 0)),
        ],
        out_specs=[
            pl.BlockSpec((tm, ncol), lambda i: (i, 0)),
            pl.BlockSpec((tm, 3 * LANES), lambda i: (i, 0)),
        ],
        out_shape=[
            jax.ShapeDtypeStruct((t, ncol), BF16),
            jax.ShapeDtypeStruct((t, 3 * LANES), F32),
        ],
        compiler_params=_cparams("parallel"),
        name="proj",
    )(h2, g, w, tab)


def _topk_rows(score, k):
    rowf = _iota(score.shape, 0).astype(F32)
    for _ in range(k):
        m = jnp.max(score, axis=0, keepdims=True)
        idx = jnp.min(jnp.where(score == m, rowf, 1e9), axis=0, keepdims=True)
        score = jnp.where(rowf == idx, BELOW_NEG, score)
    return score == BELOW_NEG


def _moba_kernel(q_ref, k_ref, e_ref, vt_ref, o_ref, km_ref, acc_ref, sa_ref, sb_ref, *, nb, topk, per_tile):
    c = pl.program_id(2)
    tq = q_ref.shape[1]
    L = MOBA_BLOCK
    tk = per_tile * L

    @pl.when(c == 0)
    def _():
        km_ref[...] = jnp.zeros_like(km_ref)
        for j in range(nb):
            km = jnp.mean(k_ref[0, j * L:(j + 1) * L, :].astype(F32), axis=0, keepdims=True)
            hi = km.astype(BF16).astype(F32)
            km_ref[0, j:j + 1, :] = hi
            km_ref[1, j:j + 1, :] = km - hi

    drow = _iota((LANES, 1), 0)
    q2t = q_ref[0].astype(F32).T
    blk = _iota((LANES, tq), 0)
    own = _div_pow2(c * tq + _iota((1, tq), 1), L)
    kmh = km_ref[0].astype(BF16)
    kml = km_ref[1].astype(BF16)
    qts, biases = [], []
    for h in range(2):
        qt = jnp.where(_div_pow2(drow, HEAD_DIM) == h, q2t, 0.0)
        qb = qt.astype(BF16)
        gate = _dot(kmh, qb) + _dot(kml, qb)
        sel = _topk_rows(jnp.where(blk < own, gate, NEG), topk)
        biases.append(jnp.where(sel | (blk == own), 0.0, NEG).astype(BF16))
        qts.append((qt * (HEAD_DIM ** -0.5 * LOG2E)).astype(BF16))
    qa = jnp.concatenate([jnp.concatenate(qts, axis=1), jnp.concatenate(biases, axis=1)], axis=0)

    def scores(t):
        k0 = pl.multiple_of(t * tk, tk)
        return _dot(jnp.concatenate([k_ref[0, pl.ds(k0, tk), :], e_ref[pl.ds(k0, tk), :]], axis=1), qa)

    def tile(t, m, l, cur_ref, nxt_ref, last):
        st = cur_ref[...]
        if last:
            kpos = t * tk + _iota((tk, tq), 0)
            qpos = c * tq + _iota((tk, tq), 1)
            tri = jnp.where(kpos <= qpos, 0.0, NEG)
            st = st + jnp.concatenate([tri, tri], axis=1)
        else:
            nxt_ref[...] = scores(t + 1)
        m_new = jnp.maximum(m, jnp.max(st, axis=0, keepdims=True))
        alpha = jnp.exp2(m - m_new)
        p = jnp.exp2(st - m_new)
        acc_ref[...] = alpha * acc_ref[...] + _dot(vt_ref[0, 0, t], p.astype(BF16))
        return m_new, alpha * l + jnp.sum(p, axis=0, keepdims=True)

    acc_ref[...] = jnp.zeros_like(acc_ref)
    sa_ref[...] = scores(0)
    n_full = ((c + 1) * (tq // L) - 1) // per_tile

    def pair(u, ml):
        ml = tile(2 * u, ml[0], ml[1], sa_ref, sb_ref, False)
        return tile(2 * u + 1, ml[0], ml[1], sb_ref, sa_ref, False)

    init = (jnp.full((1, 2 * tq), NEG, F32), jnp.zeros((1, 2 * tq), F32))
    ml = lax.fori_loop(0, n_full // 2, pair, init)

    def odd_tail(ml):
        ml = tile(n_full - 1, ml[0], ml[1], sa_ref, sb_ref, False)
        return tile(n_full, ml[0], ml[1], sb_ref, sa_ref, True)

    m, l = lax.cond(n_full % 2 == 1, odd_tail, lambda ml: tile(n_full, ml[0], ml[1], sa_ref, sb_ref, True), ml)
    o = acc_ref[...] / l
    o_ref[0] = jnp.where(drow < HEAD_DIM, o[:, 0:tq], o[:, tq:2 * tq]).T.astype(o_ref.dtype)


def _key_major(a3, blk):
    b, s, c = a3.shape
    return a3.reshape(b, s // blk, blk, c // LANES, LANES).transpose(0, 3, 1, 4, 2)


def _block_onehot(s, blk):
    return jnp.asarray(np.arange(s)[:, None] // blk == np.arange(LANES)[None, :], BF16)


def _moba(qkv, b, s, per_tile=4):
    nb = s // MOBA_BLOCK
    tq = 2 * MOBA_BLOCK
    assert nb % per_tile == 0 and nb <= LANES and per_tile % (tq // MOBA_BLOCK) == 0 and s % tq == 0
    topk = max(1, min(MOBA_TOPK, nb - 1))
    tk = per_tile * MOBA_BLOCK
    qkv3 = qkv.reshape(b, s, -1)
    gq, gk, gv = G_MOBA_Q, G_MOBA_K, G_MOBA_V
    vt = _key_major(qkv3[..., gv * LANES:(gv + 2) * LANES], tk)
    return pl.pallas_call(
        functools.partial(_moba_kernel, nb=nb, topk=topk, per_tile=per_tile),
        grid=(b, 2, s // tq),
        in_specs=[
            pl.BlockSpec((1, tq, LANES), lambda i, p, c: (i, c, gq + p)),
            pl.BlockSpec((1, s, LANES), lambda i, p, c: (i, 0, gk + p)),
            pl.BlockSpec((s, LANES), lambda i, p, c: (0, 0)),
            pl.BlockSpec((1, 1, s // tk, LANES, tk), lambda i, p, c: (i, p, 0, 0, 0)),
        ],
        out_specs=pl.BlockSpec((1, tq, LANES), lambda i, p, c: (i, c, p)),
        out_shape=jax.ShapeDtypeStruct((b, s, 2 * LANES), BF16),
        scratch_shapes=[pltpu.VMEM((2, LANES, LANES), F32), pltpu.VMEM((LANES, 2 * tq), F32),
                        pltpu.VMEM((tk, 2 * tq), F32), pltpu.VMEM((tk, 2 * tq), F32)],
        compiler_params=_cparams("parallel", "parallel", "arbitrary"),
        name="moba",
    )(qkv3, qkv3, _block_onehot(s, MOBA_BLOCK), vt)


def _cmp_kernel(x_ref, pe_ref, w1_ref, w2_ref, tab_ref, o_ref, *, rope):
    rows = x_ref.shape[2]
    half = w1_ref.shape[0] // 2
    outs = []
    for g in range(2):
        x = x_ref[0, g]
        xa = (x + pe_ref[0:1, :]).astype(BF16)
        xb = (x + pe_ref[1:2, :]).astype(BF16)
        a = _dot(xa, w1_ref[0:half, :])
        bm = _dot(xb, w1_ref[half:2 * half, :])
        h1 = a + pltpu.roll(bm, rows - 1, 0)
        outs.append(_dot(_gelu_tanh(h1).astype(BF16), w2_ref[...]))
    y = jnp.concatenate(outs, axis=1)
    if rope:
        y = _rope(y, tab_ref[0])
    o_ref[0] = y.astype(o_ref.dtype)


def _compress(x4, pe2, w1, w2, tab, rope):
    b, g, rows, width = x4.shape
    hid = w1.shape[1]
    return pl.pallas_call(
        functools.partial(_cmp_kernel, rope=rope),
        grid=(b,),
        in_specs=[
            pl.BlockSpec((1, g, rows, width), lambda i: (i, 0, 0, 0)),
            pl.BlockSpec((2, width), lambda i: (0, 0)),
            pl.BlockSpec((2 * width, hid), lambda i: (0, 0)),
            pl.BlockSpec((hid, HEAD_DIM), lambda i: (0, 0)),
            pl.BlockSpec((1, rows, 3 * LANES), lambda i: (i, 0, 0)),
        ],
        out_specs=pl.BlockSpec((1, rows, LANES), lambda i: (i, 0, 0)),
        out_shape=jax.ShapeDtypeStruct((b, rows, LANES), BF16),
        compiler_params=_cparams("parallel"),
        name="nsa_compress",
    )(x4, pe2, w1, w2, tab)


def _nsa_kernel(q_ref, kc_ref, vct_ref, ks_ref, e_ref, vst_ref, kw_ref, vwt_ref, gt_ref, ovt_ref, o_ref,
                gts_ref, acc_ref, ow_ref, sa_ref, sb_ref, *, topn, tk):
    g = pl.program_id(1)
    c = pl.program_id(2)
    tq = q_ref.shape[1]
    nrep = q_ref.shape[2] // HEAD_DIM
    ncp = kc_ref.shape[1]
    q0 = c * tq
    nwin = (NSA_WINDOW + tq) // LANES

    drow = _iota((LANES, 1), 0)
    in_g = _div_pow2(drow, HEAD_DIM) == g
    q2 = q_ref[0].astype(F32)
    cols = []
    for r in range(nrep):
        qpt = q2[:, (r // 2) * LANES:(r // 2 + 1) * LANES].T
        qpt = jnp.where(g == r % 2, qpt, pltpu.roll(qpt, HEAD_DIM, 0))
        cols.append(jnp.where(in_g, qpt, 0.0) * (HEAD_DIM ** -0.5 * LOG2E))
    qst = jnp.concatenate(cols, axis=1).astype(BF16)
    qpos = q0 + _iota((1, tq), 1)
    qpos_w = jnp.concatenate([qpos] * nrep, axis=1)

    s_c = _dot(kc_ref[0], qst)
    cmp_end = _iota((ncp, 1), 0) * NSA_CMP_STRIDE + (NSA_CMP_LEN - 1)
    mask_c = cmp_end <= qpos_w
    sm = jnp.where(mask_c, s_c, NEG)
    m = jnp.max(sm, axis=0, keepdims=True)
    e = jnp.where(mask_c, jnp.exp2(sm - m), 0.0)
    l = jnp.sum(e, axis=0, keepdims=True)
    p_c = e / jnp.where(l > 0.0, l, 1.0)

    psum = p_c[:, 0:tq]
    for r in range(1, nrep):
        psum = psum + p_c[:, r * tq:(r + 1) * tq]
    hi = psum.astype(BF16)
    lo = (psum - hi.astype(F32)).astype(BF16)
    imp = _dot(ovt_ref[...], hi) + _dot(ovt_ref[...], lo)
    o_c = _dot(vct_ref[0], p_c.astype(BF16))
    jb = _iota((LANES, tq), 0)
    cur = _div_pow2(qpos, NSA_SLC_BLOCK)
    forced = (jb == 0) | (jb == cur) | (jb == cur - 1)
    imp = jnp.where(forced, NSA_FORCE_SCORE, imp)
    imp = jnp.where(jb <= cur, imp, NEG)
    bias = jnp.where(_topk_rows(imp, topn), 0.0, NEG).astype(BF16)

    st0 = pl.multiple_of(jnp.maximum(q0 - NSA_WINDOW, 0), LANES)
    dlt = qpos_w - (st0 + _iota((nwin * LANES, 1), 0))
    s_w = jnp.where((dlt >= 0) & (dlt < NSA_WINDOW), _dot(kw_ref[0, pl.ds(st0, nwin * LANES), :], qst), NEG)
    p_w = jnp.exp2(s_w - jnp.max(s_w, axis=0, keepdims=True))
    blk0 = st0 // LANES
    o_w = _dot(vwt_ref[0, blk0], p_w[0:LANES].astype(BF16))
    for i in range(1, nwin):
        o_w = o_w + _dot(vwt_ref[0, blk0 + i], p_w[i * LANES:(i + 1) * LANES].astype(BF16))
    ow_ref[...] = o_w / jnp.sum(p_w, axis=0, keepdims=True)

    qa = jnp.concatenate([qst, jnp.concatenate([bias] * nrep, axis=1)], axis=0)

    def scores(t):
        k0 = pl.multiple_of(t * tk, tk)
        return _dot(jnp.concatenate([ks_ref[0, pl.ds(k0, tk), :], e_ref[pl.ds(k0, tk), :]], axis=1), qa)

    def tile(t, m, cur_ref, nxt_ref, diagonal):
        st = cur_ref[...]
        if diagonal:
            st = jnp.where(t * tk + _iota((tk, 1), 0) <= qpos_w, st, NEG)
        else:
            nxt_ref[...] = scores(t + 1)
        m_new = jnp.maximum(m, jnp.max(st, axis=0, keepdims=True))
        p = jnp.exp2(st - m_new)
        acc_ref[...] = jnp.exp2(m - m_new) * acc_ref[...] + _dot(vst_ref[0, 0, t], p.astype(BF16))
        return m_new

    acc_ref[...] = jnp.zeros_like(acc_ref)
    sa_ref[...] = scores(0)
    n_full = q0 // tk

    def pair(u, m):
        return tile(2 * u + 1, tile(2 * u, m, sa_ref, sb_ref, False), sb_ref, sa_ref, False)

    m_s = lax.fori_loop(0, n_full // 2, pair, jnp.full((1, nrep * tq), NEG, F32))

    def odd_tail(m):
        return tile(n_full, tile(n_full - 1, m, sa_ref, sb_ref, False), sb_ref, sa_ref, True)

    lax.cond(n_full % 2 == 1, odd_tail, lambda m: tile(n_full, m, sa_ref, sb_ref, True), m_s)
    acc = acc_ref[...]
    o_own = acc[0:HEAD_DIM] / acc[HEAD_DIM:HEAD_DIM + 1]
    gap = jnp.zeros_like(o_own)
    o_s = jnp.where(g == 0, jnp.concatenate([o_own, gap], axis=0), jnp.concatenate([gap, o_own], axis=0))
    o_w = ow_ref[...]

    gts_ref[...] = gt_ref[0].T
    outs = []
    for r in range(nrep):
        base = (g * nrep + r) * 3
        cs = slice(r * tq, (r + 1) * tq)
        o_r = (gts_ref[pl.ds(base, 1), :] * o_c[:, cs] + gts_ref[pl.ds(base + 1, 1), :] * o_s[:, cs]
               + gts_ref[pl.ds(base + 2, 1), :] * o_w[:, cs])
        outs.append(jnp.where(g == r % 2, o_r, pltpu.roll(o_r, HEAD_DIM, 0)))
    pairs = [jnp.where(drow < HEAD_DIM, outs[2 * i], outs[2 * i + 1]).T for i in range(nrep // 2)]
    o_ref[0] = jnp.concatenate(pairs, axis=1).astype(o_ref.dtype)


def _nsa_overlap_t(s, ncp):
    nc = (s - NSA_CMP_LEN) // NSA_CMP_STRIDE + 1
    nsb = s // NSA_SLC_BLOCK
    cs = np.arange(nc) * NSA_CMP_STRIDE
    bs = np.arange(nsb) * NSA_SLC_BLOCK
    ov = (cs[None, :] < bs[:, None] + NSA_SLC_BLOCK) & (cs[None, :] + NSA_CMP_LEN > bs[:, None])
    out = np.zeros((LANES, ncp), np.float32)
    out[:nsb, :nc] = ov
    return jnp.asarray(out, BF16)


def _nsa(qkv, aux, kc, vc, b, s, tq=256, tk=512):
    qkv3 = qkv.reshape(b, s, -1)
    aux3 = aux.reshape(b, s, -1)
    ncp = kc.shape[1]
    nsb = s // NSA_SLC_BLOCK
    assert nsb <= LANES and s >= NSA_WINDOW + tq and s % tk == 0 and tk % tq == 0
    topn = min(NSA_SLC_TOPN, nsb)
    vct = vc.transpose(0, 2, 1)
    vst = _key_major(qkv3[..., G_NSA_VS * LANES:(G_NSA_VS + 1) * LANES], tk)[:, 0]
    vst = vst.reshape(b, s // tk, 2, HEAD_DIM, tk).transpose(0, 2, 1, 3, 4)
    pad = jnp.zeros((b, 2, s // tk, NSA_V_ROWS - HEAD_DIM, tk), BF16).at[:, :, :, 0].set(1.0)
    vst = jnp.concatenate([vst, pad], axis=3)
    vwt = _key_major(qkv3[..., G_NSA_VW * LANES:(G_NSA_VW + 1) * LANES], LANES)[:, 0]
    full = lambda grp: pl.BlockSpec((1, s, LANES), lambda i, g, c: (i, 0, grp))
    return pl.pallas_call(
        functools.partial(_nsa_kernel, topn=topn, tk=tk),
        grid=(b, 2, s // tq),
        in_specs=[
            pl.BlockSpec((1, tq, 2 * LANES), lambda i, g, c: (i, c, G_NSA_Q // 2 + g)),
            pl.BlockSpec((1, ncp, LANES), lambda i, g, c: (i, 0, 0)),
            pl.BlockSpec((1, LANES, ncp), lambda i, g, c: (i, 0, 0)),
            full(G_NSA_KS),
            pl.BlockSpec((s, LANES), lambda i, g, c: (0, 0)),
            pl.BlockSpec((1, 1, s // tk, NSA_V_ROWS, tk), lambda i, g, c: (i, g, 0, 0, 0)),
            full(G_NSA_KW),
            pl.BlockSpec((1, s // LANES, LANES, LANES), lambda i, g, c: (i, 0, 0, 0)),
            pl.BlockSpec((1, tq, LANES), lambda i, g, c: (i, c, 2)),
            pl.BlockSpec((LANES, ncp), lambda i, g, c: (0, 0)),
        ],
        out_specs=pl.BlockSpec((1, tq, 2 * LANES), lambda i, g, c: (i, c, g)),
        out_shape=jax.ShapeDtypeStruct((b, s, 4 * LANES), BF16),
        scratch_shapes=[pltpu.VMEM((LANES, tq), F32), pltpu.VMEM((NSA_V_ROWS, 4 * tq), F32),
                        pltpu.VMEM((LANES, 4 * tq), F32), pltpu.VMEM((tk, 4 * tq), F32),
                        pltpu.VMEM((tk, 4 * tq), F32)],
        compiler_params=_cparams("parallel", "parallel", "arbitrary"),
        name="nsa",
    )(qkv3, kc, vct, qkv3, _block_onehot(s, NSA_SLC_BLOCK), vst, qkv3, vwt, aux3, _nsa_overlap_t(s, ncp))


def _sb_kernel(q_ref, k_ref, v_ref, u_ref, o_ref, *, tk):
    c = pl.program_id(2)
    tq = q_ref.shape[1]
    lane = _iota((1, LANES), 1)
    q2 = q_ref[0]
    qhs = [jnp.where(_div_pow2(lane, HEAD_DIM) == h, q2, jnp.zeros_like(q2)) * jnp.asarray(HEAD_DIM ** -0.5, BF16)
           for h in range(2)]
    qpos = c * tq + _iota((tq, tk), 0)
    kcol = _iota((tq, tk), 1)
    suffix = u_ref[...]

    def cond(st):
        return (st[0] > 0) & (st[1] < SB_SKIP_TAIL)

    def body(st):
        hi = st[0]
        lo = pl.multiple_of(jnp.maximum(hi - tk, 0), tq)
        kj = k_ref[0, pl.ds(lo, tk), :]
        vj = v_ref[0, pl.ds(lo, tk), :]
        kpos = lo + kcol
        valid = (kpos < qpos) & (kpos < hi)
        zs = [_dot_nt(qhs[h], kj) for h in range(2)]
        sps = [jnp.where(valid, jnp.maximum(z, 0.0) + jnp.log(1.0 + jnp.exp(-jnp.abs(z))), 0.0) for z in zs]
        incls = [_dot_split(sp, suffix) for sp in sps]
        ws = [jnp.where(valid, jnp.exp(zs[h] - incls[h] - st[2 + 2 * h]), 0.0) for h in range(2)]
        out = []
        for h in range(2):
            out += [st[2 + 2 * h] + incls[h][:, 0:1], st[3 + 2 * h] + _dot(ws[h].astype(BF16), vj)]
        return (lo, jnp.minimum(jnp.min(out[0]), jnp.min(out[2])), *out)

    zero_t = jnp.zeros((tq, 1), F32)
    zero_a = jnp.zeros((tq, LANES), F32)
    res = lax.while_loop(cond, body, ((c + 1) * tq, jnp.float32(0.0), zero_t, zero_a, zero_t, zero_a))
    o_ref[0] = jnp.where(lane < HEAD_DIM, res[3], res[5]).astype(o_ref.dtype)


def _sb(qkv, b, s, tq=256, tk=512):
    assert s >= tk and tk % tq == 0
    qkv3 = qkv.reshape(b, s, -1)
    gq, gk, gv = G_SB_Q, G_SB_K, G_SB_V
    suffix = jnp.asarray(np.arange(tk)[:, None] >= np.arange(tk)[None, :], BF16)
    return pl.pallas_call(
        functools.partial(_sb_kernel, tk=tk),
        grid=(b, 2, s // tq),
        in_specs=[
            pl.BlockSpec((1, tq, LANES), lambda i, p, c: (i, c, gq + p)),
            pl.BlockSpec((1, s, LANES), lambda i, p, c: (i, 0, gk + p)),
            pl.BlockSpec((1, s, LANES), lambda i, p, c: (i, 0, gv + p)),
            pl.BlockSpec((tk, tk), lambda i, p, c: (0, 0)),
        ],
        out_specs=pl.BlockSpec((1, tq, LANES), lambda i, p, c: (i, c, p)),
        out_shape=jax.ShapeDtypeStruct((b, s, 2 * LANES), BF16),
        compiler_params=_cparams("parallel", "parallel", "arbitrary"),
        name="stick_breaking",
    )(qkv3, qkv3, qkv3, suffix)


def _outproj_kernel(h_ref, om_ref, on_ref, os_ref, w_ref, g_ref, o_ref):
    dm, dn = om_ref.shape[1], on_ref.shape[1]
    mix = (_dot(om_ref[...], w_ref[0:dm, :]) + _dot(on_ref[...], w_ref[dm:dm + dn, :])
           + _dot(os_ref[...], w_ref[dm + dn:, :]))
    o_ref[...] = h_ref[...] + _rms(mix, g_ref[...])


def _outproj(h2, om, on, osb, w, g, tm=1024):
    t, d = h2.shape
    row = lambda n: pl.BlockSpec((tm, n), lambda i: (i, 0))
    return pl.pallas_call(
        _outproj_kernel,
        grid=(t // tm,),
        in_specs=[row(d), row(om.shape[1]), row(on.shape[1]), row(osb.shape[1]),
                  pl.BlockSpec(w.shape, lambda i: (0, 0)), pl.BlockSpec((1, d), lambda i: (0, 0))],
        out_specs=row(d),
        out_shape=jax.ShapeDtypeStruct((t, d), F32),
        compiler_params=_cparams("parallel"),
        name="mixer_out",
    )(h2, om, on, osb, w, g)


def _memkv_kernel(m_ref, g_ref, w_ref, o_ref):
    o_ref[...] = _dot(_rms(m_ref[...], g_ref[...]).astype(BF16), w_ref[...]).astype(o_ref.dtype)


def _memkv(mem2, g, wkv, tm=256):
    t, d = mem2.shape
    n = wkv.shape[1]
    return pl.pallas_call(
        _memkv_kernel,
        grid=(t // tm,),
        in_specs=[pl.BlockSpec((tm, d), lambda i: (i, 0)), pl.BlockSpec((1, d), lambda i: (0, 0)),
                  pl.BlockSpec((d, n), lambda i: (0, 0))],
        out_specs=pl.BlockSpec((tm, n), lambda i: (i, 0)),
        out_shape=jax.ShapeDtypeStruct((t, n), BF16),
        compiler_params=_cparams("parallel"),
        name="mem_kv",
    )(mem2, g, wkv)


def _cross_kernel(h_ref, k_ref, v_ref, wq_ref, wo_ref, gpre_ref, gpost_ref, o_ref):
    h = h_ref[0]
    d = h.shape[1]
    hd = d // X_HEADS
    q = _dot(_rms(h, gpre_ref[...]).astype(BF16), wq_ref[...])
    cols = [slice(x * hd, (x + 1) * hd) for x in range(X_HEADS)]
    ss = [_dot_nt((q[:, c] * hd ** -0.5).astype(BF16), k_ref[0, :, c]) for c in cols]
    ps = []
    for s in ss:
        p = jnp.exp(s - jnp.max(s, axis=1, keepdims=True))
        ps.append((p / jnp.sum(p, axis=1, keepdims=True)).astype(BF16))
    outs = [_dot(p, v_ref[0, :, c]).astype(BF16) for p, c in zip(ps, cols)]
    o = _dot(jnp.concatenate(outs, axis=1), wo_ref[...])
    o_ref[0] = h + _rms(o, gpost_ref[...])


def _cross(h3, kv3, wq, wo, gpre, gpost, tm=512):
    b, s, d = h3.shape
    m = kv3.shape[1]
    vec = pl.BlockSpec((1, d), lambda i, j: (0, 0))
    mat = pl.BlockSpec((d, d), lambda i, j: (0, 0))
    return pl.pallas_call(
        _cross_kernel,
        grid=(b, s // tm),
        in_specs=[pl.BlockSpec((1, tm, d), lambda i, j: (i, j, 0)),
                  pl.BlockSpec((1, m, d), lambda i, j: (i, 0, 0)),
                  pl.BlockSpec((1, m, d), lambda i, j: (i, 0, 1)),
                  mat, mat, vec, vec],
        out_specs=pl.BlockSpec((1, tm, d), lambda i, j: (i, j, 0)),
        out_shape=jax.ShapeDtypeStruct((b, s, d), F32),
        compiler_params=_cparams("parallel", "parallel"),
        name="cross_attn",
    )(h3, kv3, kv3, wq, wo, gpre, gpost)


def _ffn_kernel(h_ref, wup_ref, cw_ref, cb_ref, wdn_ref, gpre_ref, gpost_ref, o_ref, tail_ref, acc_ref,
                *, tc):
    j = pl.program_id(1)
    h = h_ref[0]
    tm = h.shape[0]
    dff = wdn_ref.shape[0]

    @pl.when(j == 0)
    def _():
        tail_ref[...] = jnp.zeros_like(tail_ref)

    xn = _rms(h, gpre_ref[...]).astype(BF16)
    row = _iota((tm, tc), 0)

    def up(c0):
        return _dot(xn, wup_ref[:, c0:c0 + tc]), _dot(xn, wup_ref[:, dff + c0:dff + c0 + tc])

    nxt = up(0)
    for c0 in range(0, dff, tc):
        cols = slice(c0, c0 + tc)
        a, gate = nxt
        if c0 + tc < dff:
            nxt = up(c0 + tc)
        prev1 = tail_ref[7:8, cols]
        prev2 = tail_ref[6:7, cols]
        a1 = jnp.where(row == 0, prev1, pltpu.roll(a, 1, 0))
        a2 = jnp.where(row == 0, prev2, jnp.where(row == 1, prev1, pltpu.roll(a, 2, 0)))
        tail_ref[:, cols] = a[tm - 8:tm, :]
        y = cw_ref[0:1, cols] * a2 + cw_ref[1:2, cols] * a1 + cw_ref[2:3, cols] * a + cb_ref[:, cols]
        part = _dot((_gelu_tanh(y) * gate).astype(BF16), wdn_ref[cols, :])
        if c0 == 0:
            acc_ref[...] = part
        else:
            acc_ref[...] += part
    o_ref[0] = h + _rms(acc_ref[...], gpost_ref[...])


def _ffn(h3, wup, cw, cb, wdn, gpre, gpost, tm=256, tc=256):
    b, s, d = h3.shape
    dff = wdn.shape[0]
    const = lambda shape: pl.BlockSpec(shape, lambda i, j: (0, 0))
    return pl.pallas_call(
        functools.partial(_ffn_kernel, tc=tc),
        grid=(b, s // tm),
        in_specs=[pl.BlockSpec((1, tm, d), lambda i, j: (i, j, 0)),
                  const(wup.shape), const(cw.shape), const(cb.shape), const(wdn.shape),
                  const((1, d)), const((1, d))],
        out_specs=pl.BlockSpec((1, tm, d), lambda i, j: (i, j, 0)),
        out_shape=jax.ShapeDtypeStruct((b, s, d), F32),
        scratch_shapes=[pltpu.VMEM((8, dff), F32), pltpu.VMEM((tm, d), F32)],
        compiler_params=_cparams("parallel", "arbitrary"),
        name="conv_ffn",
    )(h3, wup, cw, cb, wdn, gpre, gpost)


def _rope_tables(pos):
    inv_freq = ROPE_THETA ** (-jnp.arange(ROT_HALF, dtype=F32) / ROT_HALF)
    ang = pos.astype(F32)[..., None] * inv_freq
    cs = jnp.concatenate([jnp.cos(ang), jnp.sin(ang)], axis=-1)
    dim = np.arange(LANES) % HEAD_DIM
    spread = np.zeros((2 * ROT_HALF, 3 * LANES), np.float32)
    ones = np.zeros((3 * LANES,), np.float32)
    for lane in range(LANES):
        d = dim[lane]
        if d < ROT_HALF:
            spread[d, lane] = 1.0
            spread[ROT_HALF + d, LANES + lane] = -1.0
        elif d < 2 * ROT_HALF:
            spread[d - ROT_HALF, lane] = 1.0
            spread[d, 2 * LANES + lane] = 1.0
        else:
            ones[lane] = 1.0
    return jnp.dot(cs, jnp.asarray(spread), precision=lax.Precision.HIGHEST) + jnp.asarray(ones)


def _pad_gate_cols(w_in):
    cut = G_GATE * LANES
    pad = jnp.zeros((w_in.shape[0], LANES - GATE_COLS), w_in.dtype)
    return jnp.concatenate([w_in[:, :cut + GATE_COLS], pad, w_in[:, cut + GATE_COLS:]], axis=1)


def kernel(x, mem, positions, norm_mix_pre, norm_mix_post, w_in, w_out, cmp_pe_k, cmp_pe_v, cmp_wk1, cmp_wk2, cmp_wv1, cmp_wv2, norm_x_pre, norm_x_post, norm_mem, x_wq, x_wk, x_wv, x_wo, norm_ffn_pre, norm_ffn_post, ffn_w_up, ffn_conv_w, ffn_conv_b, ffn_w_down):
    b, s, d = x.shape
    depth = w_in.shape[0]
    t = b * s
    rows = s // NSA_CMP_STRIDE
    tab_tok = _rope_tables(positions).reshape(t, 3 * LANES)
    end_idx = jnp.minimum(jnp.arange(rows) * NSA_CMP_STRIDE + NSA_CMP_LEN - 1, s - 1)
    tab_end = _rope_tables(positions[:, end_idx])
    vec = lambda g: g.reshape(1, -1)

    def blocks16(a):
        return a.reshape(b, s, 2, HEAD_DIM).transpose(0, 2, 1, 3).reshape(b, 2, rows, NSA_CMP_STRIDE * HEAD_DIM)

    h = x
    for l in range(depth):
        qkv, aux = _proj(h.reshape(t, d), vec(norm_mix_pre[l]), _pad_gate_cols(w_in[l]).astype(BF16), tab_tok)
        aux3 = aux.reshape(b, s, 3 * LANES)
        pe_k = cmp_pe_k[l].reshape(2, -1)
        pe_v = cmp_pe_v[l].reshape(2, -1)
        kc = _compress(blocks16(aux3[..., 0:LANES]), pe_k, cmp_wk1[l].astype(BF16), cmp_wk2[l].astype(BF16),
                       tab_end, True)
        vc = _compress(blocks16(aux3[..., LANES:2 * LANES]), pe_v, cmp_wv1[l].astype(BF16),
                       cmp_wv2[l].astype(BF16), tab_end, False)
        o_moba = _moba(qkv, b, s)
        o_nsa = _nsa(qkv, aux, kc, vc, b, s)
        o_sb = _sb(qkv, b, s)
        h = _outproj(h.reshape(t, d), o_moba.reshape(t, -1), o_nsa.reshape(t, -1), o_sb.reshape(t, -1),
                     w_out[l].astype(BF16), vec(norm_mix_post[l]))
        wkv = jnp.concatenate([x_wk[l], x_wv[l]], axis=1).astype(BF16)
        kv = _memkv(mem.reshape(-1, d), vec(norm_mem[l]), wkv).reshape(b, -1, 2 * d)
        h = _cross(h.reshape(b, s, d), kv, x_wq[l].astype(BF16), x_wo[l].astype(BF16),
                   vec(norm_x_pre[l]), vec(norm_x_post[l]))
        h = _ffn(h, ffn_w_up[l].astype(BF16), ffn_conv_w[l], vec(ffn_conv_b[l]), ffn_w_down[l].astype(BF16),
                 vec(norm_ffn_pre[l]), vec(norm_ffn_post[l]))
    return h
```
